```python
import jax, jax.numpy as jnp
from jax import lax
import numpy as np

D_MODEL = 1024
BATCH = 8
SEQ = 2048
DEPTH = 4
DEC_BATCH = 128
DEC_SEQ = 8
PAST_LEN = 16384
PAGE_SIZE = 128

N_MIXERS = 2
N_A = (DEPTH + 1) // 2
N_B = DEPTH // 2
CONV_A_WIDTH = 31
CONV_B_WIDTH = 3
D_FF = -(-(8 * D_MODEL) // (3 * 256)) * 256
RMS_EPS = 1e-6
LN_EPS = 1e-5

kernel_name = 'hybrid_conformer_shortconv_decoder_step'


def rmsnorm(x, g):
    xf = x.astype(jnp.float32)
    y = xf * lax.rsqrt(jnp.mean(xf * xf, axis=-1, keepdims=True) + RMS_EPS)
    return (y * g.astype(jnp.float32)).astype(x.dtype)


def layernorm(x, g, b):
    xf = x.astype(jnp.float32)
    mu = jnp.mean(xf, axis=-1, keepdims=True)
    xc = xf - mu
    var = jnp.mean(xc * xc, axis=-1, keepdims=True)
    y = xc * lax.rsqrt(var + LN_EPS) * g.astype(jnp.float32) + b.astype(jnp.float32)
    return y.astype(x.dtype)


def causal_dwconv(h_ext, w):
    c = h_ext.shape[-1]
    return lax.conv_general_dilated(
        h_ext, w[:, None, :].astype(h_ext.dtype), window_strides=(1,), padding='VALID',
        dimension_numbers=('NWC', 'WIO', 'NWC'), feature_group_count=c)


def mixer_a(h, buf, w_in, b_in, w_dw, b_dw, ln_g, ln_b, w_out, b_out):
    u = h @ w_in + b_in
    a, g = jnp.split(u, 2, axis=-1)
    v = a * jax.nn.sigmoid(g)
    v_ext = jnp.concatenate([buf.astype(v.dtype), v], axis=1)
    c = causal_dwconv(v_ext, w_dw) + b_dw
    c = layernorm(c, ln_g, ln_b)
    y = jax.nn.silu(c) @ w_out + b_out
    return y, v_ext[:, -(CONV_A_WIDTH - 1):]


def mixer_b(h, buf, w_in, w_conv, w_out):
    u = h @ w_in
    bg, cg, v = jnp.split(u, 3, axis=-1)
    z = cg * v
    z_ext = jnp.concatenate([buf.astype(z.dtype), z], axis=1)
    c = causal_dwconv(z_ext, w_conv)
    y = (bg * c) @ w_out
    return y, z_ext[:, -(CONV_B_WIDTH - 1):]


def swiglu(h, w_gate_up, w_down):
    gate, up = jnp.split(h @ w_gate_up, 2, axis=-1)
    return (jax.nn.silu(gate) * up) @ w_down


def trunk(x, bufs_a, bufs_b, g_mix_pre, g_mix_post, g_ffn_pre, g_ffn_post,
          a_w_in, a_b_in, a_w_dw, a_b_dw, a_ln_g, a_ln_b, a_w_out, a_b_out,
          b_w_in, b_w_conv, b_w_out, w_gate_up, w_down):
    new_a, new_b = [], []
    for i in range(DEPTH):
        h = rmsnorm(x, g_mix_pre[i])
        j = i // N_MIXERS
        if i % N_MIXERS == 0:
            m, nb = mixer_a(h, bufs_a[j], a_w_in[j], a_b_in[j], a_w_dw[j], a_b_dw[j],
                            a_ln_g[j], a_ln_b[j], a_w_out[j], a_b_out[j])
            new_a.append(nb)
        else:
            m, nb = mixer_b(h, bufs_b[j], b_w_in[j], b_w_conv[j], b_w_out[j])
            new_b.append(nb)
        x = x + rmsnorm(m, g_mix_post[i])
        f = swiglu(rmsnorm(x, g_ffn_pre[i]), w_gate_up[i], w_down[i])
        x = x + rmsnorm(f, g_ffn_post[i])
    return x, jnp.stack(new_a), jnp.stack(new_b)


def setup_inputs(seed: int = 0) -> dict:
    key = jax.random.key(seed)
    ks = jax.random.split(key, 24)
    D, F = D_MODEL, D_FF
    n = lambda k, shape, s: jax.random.normal(k, shape, jnp.float32) * s
    return {
        'x_prompt': n(ks[0], (BATCH, SEQ, D), 1.0),
        'x_sample': n(ks[1], (DEC_BATCH, DEC_SEQ, D), 1.0),
        'state_conv_a': n(ks[2], (N_A, DEC_BATCH, CONV_A_WIDTH - 1, D), 0.5),
        'state_conv_b': n(ks[3], (N_B, DEC_BATCH, CONV_B_WIDTH - 1, D), 0.5),
        'g_mix_pre': 1.0 + n(ks[4], (DEPTH, D), 0.02),
        'g_mix_post': 1.0 + n(ks[5], (DEPTH, D), 0.02),
        'g_ffn_pre': 1.0 + n(ks[6], (DEPTH, D), 0.02),
        'g_ffn_post': 1.0 + n(ks[7], (DEPTH, D), 0.02),
        'a_w_in': n(ks[8], (N_A, D, 2 * D), D ** -0.5),
        'a_b_in': n(ks[9], (N_A, 2 * D), 0.02),
        'a_w_dw': n(ks[10], (N_A, CONV_A_WIDTH, D), CONV_A_WIDTH ** -0.5),
        'a_b_dw': n(ks[11], (N_A, D), 0.02),
        'a_ln_g': 1.0 + n(ks[12], (N_A, D), 0.02),
        'a_ln_b': n(ks[13], (N_A, D), 0.02),
        'a_w_out': n(ks[14], (N_A, D, D), D ** -0.5),
        'a_b_out': n(ks[15], (N_A, D), 0.02),
        'b_w_in': n(ks[16], (N_B, D, 3 * D), D ** -0.5),
        'b_w_conv': n(ks[17], (N_B, CONV_B_WIDTH, D), CONV_B_WIDTH ** -0.5),
        'b_w_out': n(ks[18], (N_B, D, D), D ** -0.5),
        'w_gate_up': n(ks[19], (DEPTH, D, 2 * F), D ** -0.5),
        'w_down': n(ks[20], (DEPTH, F, D), F ** -0.5),
    }


def reference(x_prompt, x_sample, state_conv_a, state_conv_b,
              g_mix_pre, g_mix_post, g_ffn_pre, g_ffn_post,
              a_w_in, a_b_in, a_w_dw, a_b_dw, a_ln_g, a_ln_b, a_w_out, a_b_out,
              b_w_in, b_w_conv, b_w_out, w_gate_up, w_down):
    weights = (g_mix_pre, g_mix_post, g_ffn_pre, g_ffn_post,
               a_w_in, a_b_in, a_w_dw, a_b_dw, a_ln_g, a_ln_b, a_w_out, a_b_out,
               b_w_in, b_w_conv, b_w_out, w_gate_up, w_down)
    zeros_a = jnp.zeros((N_A, x_prompt.shape[0], CONV_A_WIDTH - 1, D_MODEL), x_prompt.dtype)
    zeros_b = jnp.zeros((N_B, x_prompt.shape[0], CONV_B_WIDTH - 1, D_MODEL), x_prompt.dtype)
    y_prompt, new_conv_a_prompt, new_conv_b_prompt = trunk(x_prompt, zeros_a, zeros_b, *weights)
    y_sample, new_conv_a_sample, new_conv_b_sample = trunk(x_sample, state_conv_a, state_conv_b, *weights)
    return (y_prompt, y_sample, new_conv_a_prompt, new_conv_b_prompt, new_conv_a_sample, new_conv_b_sample)
```

```python
import functools

import jax
import jax.numpy as jnp
from jax import lax
from jax.experimental import pallas as pl
from jax.experimental.pallas import tpu as pltpu

RMS_EPS = 1e-6
LN_EPS = 1e-5

LANES = 128
SUBLANES = 8
MXU_COLS = 256
VMEM_LIMIT_BYTES = 56 * 1024 * 1024

PROMPT_TILE = 256
PROMPT_CHUNK = 64
SAMPLE_SEQS = 32

f32 = jnp.float32
bf16 = jnp.bfloat16


def _round_up(n, m):
    return -(-n // m) * m


def _rmsnorm(x, g):
    ms = jnp.mean(x * x, axis=-1, keepdims=True)
    return x * lax.rsqrt(ms + RMS_EPS) * g


def _layernorm(x, g, b):
    mu = jnp.mean(x, axis=-1, keepdims=True)
    xc = x - mu
    var = jnp.mean(xc * xc, axis=-1, keepdims=True)
    return xc * lax.rsqrt(var + LN_EPS) * g + b


def _ffn_chunks(d_ff):
    step = 3 * MXU_COLS
    return [(a, min(a + step, d_ff)) for a in range(0, d_ff, step)]


def _swiglu_residual(x, g_pre, w_gu_ref, w_down_ref, g_post, act_ref):
    d_ff = w_down_ref.shape[0]
    h = _rmsnorm(x, g_pre).astype(bf16)
    for a, b in _ffn_chunks(d_ff):
        gate = jnp.dot(h, w_gu_ref[:, a:b], preferred_element_type=f32)
        up = jnp.dot(h, w_gu_ref[:, d_ff + a:d_ff + b], preferred_element_type=f32)
        act_ref[:, a:b] = (jax.nn.silu(gate) * up).astype(bf16)
    f = jnp.dot(act_ref[...], w_down_ref[...], preferred_element_type=f32)
    return x + _rmsnorm(f, g_post)


def _fill_ext(v, ext_ref, halo_ref, state_ref, first_step, hist):
    n_chunks, rows, d = ext_ref.shape
    pad = _round_up(hist, SUBLANES)
    chunk = rows - pad
    ext_ref[:, pad:, :] = v.reshape(n_chunks, chunk, d)
    if state_ref is None:
        @pl.when(first_step)
        def _():
            halo_ref[...] = jnp.zeros_like(halo_ref)
        ext_ref[0, 0:pad, :] = halo_ref[...]
        if n_chunks > 1:
            ext_ref[1:, 0:pad, :] = ext_ref[:n_chunks - 1, chunk:, :]
    else:
        ext_ref[:, pad - hist:pad, :] = state_ref[...]


def _emit_state(ext_ref, halo_ref, state_ref, new_state_ref, hist):
    n_chunks, rows, _ = ext_ref.shape
    pad = _round_up(hist, SUBLANES)
    chunk = rows - pad
    if state_ref is None:
        halo_ref[...] = ext_ref[n_chunks - 1, chunk:, :]
        new_state_ref[...] = ext_ref[n_chunks - 1, rows - hist:, :]
    else:
        new_state_ref[...] = ext_ref[:, rows - hist:, :]


def _causal_dwconv(ext_ref, w_ref, b_ref, out_ref, hist):
    n_chunks, rows, d = ext_ref.shape
    pad = _round_up(hist, SUBLANES)
    chunk = rows - pad
    base = pad - hist

    def body(g, carry):
        row0 = pl.multiple_of(g * chunk, SUBLANES)
        for j in range(d // LANES):
            ls = slice(j * LANES, (j + 1) * LANES)
            acc = ext_ref[g, base:base + chunk, ls] * w_ref[0:1, ls]
            for k in range(1, hist + 1):
                acc = acc + ext_ref[g, base + k:base + k + chunk, ls] * w_ref[k:k + 1, ls]
            if b_ref is not None:
                acc = acc + b_ref[:, ls]
            out_ref[pl.ds(row0, chunk), ls] = acc
        return carry

    lax.fori_loop(0, n_chunks, body, 0)


def _layer_a_kernel(*refs, per_seq_state):
    it = iter(refs)
    x_ref = next(it)
    state_ref = next(it) if per_seq_state else None
    (g_pre, w_in, b_in, w_dw, b_dw, ln_g, ln_b, w_out, b_out, g_post,
     g_fpre, w_gu, w_down, g_fpost) = [next(it) for _ in range(14)]
    y_ref, new_state_ref = next(it), next(it)
    ext_ref, c_ref, act_ref = next(it), next(it), next(it)
    halo_ref = None if per_seq_state else next(it)

    d = x_ref.shape[-1]
    hist = w_dw.shape[0] - 1
    x = x_ref[...]
    h = _rmsnorm(x, g_pre[...]).astype(bf16)
    u = jnp.dot(h, w_in[...], preferred_element_type=f32) + b_in[...]
    v = u[:, :d] * jax.nn.sigmoid(u[:, d:])

    _fill_ext(v, ext_ref, halo_ref, state_ref, pl.program_id(1) == 0, hist)
    _causal_dwconv(ext_ref, w_dw, b_dw, c_ref, hist)
    _emit_state(ext_ref, halo_ref, state_ref, new_state_ref, hist)

    c = _layernorm(c_ref[...], ln_g[...], ln_b[...])
    m = jnp.dot(jax.nn.silu(c).astype(bf16), w_out[...], preferred_element_type=f32) + b_out[...]
    x = x + _rmsnorm(m, g_post[...])
    y_ref[...] = _swiglu_residual(x, g_fpre[...], w_gu, w_down, g_fpost[...], act_ref)


def _layer_b_kernel(*refs, per_seq_state):
    it = iter(refs)
    x_ref = next(it)
    state_ref = next(it) if per_seq_state else None
    (g_pre, w_in, w_conv, w_out, g_post,
     g_fpre, w_gu, w_down, g_fpost) = [next(it) for _ in range(9)]
    y_ref, new_state_ref = next(it), next(it)
    ext_ref, c_ref, act_ref = next(it), next(it), next(it)
    halo_ref = None if per_seq_state else next(it)

    d = x_ref.shape[-1]
    hist = w_conv.shape[0] - 1
    x = x_ref[...]
    h = _rmsnorm(x, g_pre[...]).astype(bf16)
    u = jnp.dot(h, w_in[...], preferred_element_type=f32)
    bg = u[:, :d]
    z = u[:, d:2 * d] * u[:, 2 * d:]

    _fill_ext(z, ext_ref, halo_ref, state_ref, pl.program_id(1) == 0, hist)
    _causal_dwconv(ext_ref, w_conv, None, c_ref, hist)
    _emit_state(ext_ref, halo_ref, state_ref, new_state_ref, hist)

    m = jnp.dot((bg * c_ref[...]).astype(bf16), w_out[...], preferred_element_type=f32)
    x = x + _rmsnorm(m, g_post[...])
    y_ref[...] = _swiglu_residual(x, g_fpre[...], w_gu, w_down, g_fpost[...], act_ref)


def _resident(arr):
    nd = arr.ndim
    return pl.BlockSpec(arr.shape, lambda b, s: (0,) * nd, pipeline_mode=pl.Buffered(1))


def _run_layer(body, x, state, weights, hist, *, tile, chunk, name):
    n_batch, seq, d = x.shape
    d_ff = weights[-2].shape[0]
    n_chunks = tile // chunk
    pad = _round_up(hist, SUBLANES)
    per_seq_state = state is not None
    assert seq % tile == 0 and tile % chunk == 0 and chunk % SUBLANES == 0
    assert per_seq_state or chunk >= pad

    x_spec = pl.BlockSpec((None, tile, d), lambda b, s: (b, s, 0))
    in_specs = [x_spec]
    inputs = [x]
    if per_seq_state:
        assert n_batch == 1
        state_spec = pl.BlockSpec((n_chunks, hist, d), lambda b, s: (s, 0, 0))
        in_specs.append(state_spec)
        inputs.append(state)
        new_state_shape = state.shape
    else:
        state_spec = pl.BlockSpec((None, hist, d), lambda b, s: (b, 0, 0))
        new_state_shape = (n_batch, hist, d)
    in_specs += [_resident(w) for w in weights]
    inputs += list(weights)

    scratch = [
        pltpu.VMEM((n_chunks, pad + chunk, d), f32),
        pltpu.VMEM((tile, d), f32),
        pltpu.VMEM((tile, d_ff), bf16),
    ]
    if not per_seq_state:
        scratch.append(pltpu.VMEM((pad, d), f32))

    return pl.pallas_call(
        functools.partial(body, per_seq_state=per_seq_state),
        grid=(n_batch, seq // tile),
        in_specs=in_specs,
        out_specs=[x_spec, state_spec],
        out_shape=[jax.ShapeDtypeStruct(x.shape, x.dtype),
                   jax.ShapeDtypeStruct(new_state_shape, x.dtype)],
        scratch_shapes=scratch,
        compiler_params=pltpu.CompilerParams(
            dimension_semantics=("arbitrary", "arbitrary"),
            vmem_limit_bytes=VMEM_LIMIT_BYTES),
        name=name,
    )(*inputs)


def kernel(x_prompt, x_sample, state_conv_a, state_conv_b, g_mix_pre, g_mix_post, g_ffn_pre, g_ffn_post, a_w_in, a_b_in, a_w_dw, a_b_dw, a_ln_g, a_ln_b, a_w_out, a_b_out, b_w_in, b_w_conv, b_w_out, w_gate_up, w_down):
    depth = g_mix_pre.shape[0]
    dec_batch, dec_seq, d = x_sample.shape
    row = lambda vec: vec.reshape(1, -1)

    xp = x_prompt
    xs = x_sample.reshape(1, dec_batch * dec_seq, d)
    new_a_p, new_b_p, new_a_s, new_b_s = [], [], [], []
    for i in range(depth):
        j = i // 2
        ffn = (row(g_ffn_pre[i]), w_gate_up[i].astype(bf16), w_down[i].astype(bf16),
               row(g_ffn_post[i]))
        if i % 2 == 0:
            body = _layer_a_kernel
            weights = (row(g_mix_pre[i]), a_w_in[j].astype(bf16), row(a_b_in[j]),
                       a_w_dw[j], row(a_b_dw[j]), row(a_ln_g[j]), row(a_ln_b[j]),
                       a_w_out[j].astype(bf16), row(a_b_out[j]), row(g_mix_post[i])) + ffn
            state, hist = state_conv_a[j], a_w_dw.shape[1] - 1
            outs_p, outs_s = new_a_p, new_a_s
        else:
            body = _layer_b_kernel
            weights = (row(g_mix_pre[i]), b_w_in[j].astype(bf16), b_w_conv[j],
                       b_w_out[j].astype(bf16), row(g_mix_post[i])) + ffn
            state, hist = state_conv_b[j], b_w_conv.shape[1] - 1
            outs_p, outs_s = new_b_p, new_b_s
        xp, st_p = _run_layer(body, xp, None, weights, hist, tile=PROMPT_TILE,
                              chunk=PROMPT_CHUNK, name=f"layer{i}_prompt")
        xs, st_s = _run_layer(body, xs, state, weights, hist, tile=SAMPLE_SEQS * dec_seq,
                              chunk=dec_seq, name=f"layer{i}_sample")
        outs_p.append(st_p)
        outs_s.append(st_s)

    return (xp, xs.reshape(x_sample.shape), jnp.stack(new_a_p), jnp.stack(new_b_p),
            jnp.stack(new_a_s), jnp.stack(new_b_s))
```

```python
import functools

import jax
import jax.numpy as jnp
from jax import lax
from jax.experimental import pallas as pl
from jax.experimental.pallas import tpu as pltpu

RMS_EPS = 1e-6
LN_EPS = 1e-5

LANES = 128
SUBLANES = 8
MXU_COLS = 256
VMEM_LIMIT_BYTES = 56 * 1024 * 1024

PROMPT_TILE = 256
SAMPLE_SEQS = 32
CONV_ROWS = 64

f32 = jnp.float32
bf16 = jnp.bfloat16


def _round_up(n, m):
    return -(-n // m) * m


def _lanes(j):
    return slice(j * LANES, (j + 1) * LANES)


def _rmsnorm(x, g):
    ms = jnp.mean(x * x, axis=-1, keepdims=True)
    return x * lax.rsqrt(ms + RMS_EPS) * g


def _layernorm(x, g, b):
    mu = jnp.mean(x, axis=-1, keepdims=True)
    xc = x - mu
    var = jnp.mean(xc * xc, axis=-1, keepdims=True)
    return xc * lax.rsqrt(var + LN_EPS) * g + b


def _ffn_chunks(d_ff):
    step = 3 * MXU_COLS
    return [(a, min(a + step, d_ff)) for a in range(0, d_ff, step)]


def _swiglu_residual(x, g_pre, w_gu_ref, w_down_ref, g_post, act_ref):
    d_ff = w_down_ref.shape[0]
    h = _rmsnorm(x, g_pre).astype(bf16)
    for a, b in _ffn_chunks(d_ff):
        gate = jnp.dot(h, w_gu_ref[:, a:b], preferred_element_type=f32)
        up = jnp.dot(h, w_gu_ref[:, d_ff + a:d_ff + b], preferred_element_type=f32)
        act_ref[:, a:b] = (jax.nn.silu(gate) * up).astype(bf16)
    f = jnp.dot(act_ref[...], w_down_ref[...], preferred_element_type=f32)
    return x + _rmsnorm(f, g_post)


def _fill_ext(v, ext_ref, state_ref, first_step, hist):
    n_lane_blocks, n_seq, rows, _ = ext_ref.shape
    pad = _round_up(hist, SUBLANES)
    new_rows = rows - pad
    v3 = v.reshape(n_seq, new_rows, v.shape[-1])
    for j in range(n_lane_blocks):
        ext_ref[j, :, pad:, :] = v3[:, :, _lanes(j)]
    if state_ref is None:
        @pl.when(first_step)
        def _():
            ext_ref[:, :, 0:pad, :] = jnp.zeros((n_lane_blocks, n_seq, pad, LANES), f32)
    else:
        for j in range(n_lane_blocks):
            ext_ref[j, :, pad - hist:pad, :] = state_ref[:, :, _lanes(j)]


def _emit_state(ext_ref, state_ref, new_state_ref, hist):
    n_lane_blocks, n_seq, rows, _ = ext_ref.shape
    pad = _round_up(hist, SUBLANES)
    for j in range(n_lane_blocks):
        tail = ext_ref[j, :, rows - hist:, :]
        if state_ref is None:
            new_state_ref[:, _lanes(j)] = tail[0]
        else:
            new_state_ref[:, :, _lanes(j)] = tail
    if state_ref is None:
        ext_ref[:, :, 0:pad, :] = ext_ref[:, :, rows - pad:, :]


def _causal_dwconv(ext_ref, w_ref, b_ref, out_ref, hist):
    n_lane_blocks, n_seq, rows, _ = ext_ref.shape
    pad = _round_up(hist, SUBLANES)
    new_rows = rows - pad
    base = pad - hist
    block = min(CONV_ROWS, new_rows)

    def conv_block(g, out_row0, r0, j):
        ls = _lanes(j)
        acc = ext_ref[j, g, base + r0:base + r0 + block, :] * w_ref[0:1, ls]
        for k in range(1, hist + 1):
            acc = acc + ext_ref[j, g, base + r0 + k:base + r0 + k + block, :] * w_ref[k:k + 1, ls]
        if b_ref is not None:
            acc = acc + b_ref[:, ls]
        out_ref[pl.ds(out_row0, block), ls] = acc

    if n_seq == 1:
        for r0 in range(0, new_rows, block):
            for j in range(n_lane_blocks):
                conv_block(0, r0, r0, j)
    else:
        def body(g, carry):
            for j in range(n_lane_blocks):
                conv_block(g, pl.multiple_of(g * new_rows, SUBLANES), 0, j)
            return carry
        lax.fori_loop(0, n_seq, body, 0)


def _layer_a_kernel(*refs, per_seq_state):
    it = iter(refs)
    x_ref = next(it)
    state_ref = next(it) if per_seq_state else None
    (g_pre, w_in, b_in, w_dw, b_dw, ln_g, ln_b, w_out, b_out, g_post,
     g_fpre, w_gu, w_down, g_fpost) = [next(it) for _ in range(14)]
    y_ref, new_state_ref = next(it), next(it)
    ext_ref, c_ref, act_ref = next(it), next(it), next(it)

    d = x_ref.shape[-1]
    hist = w_dw.shape[0] - 1
    x = x_ref[...]
    h = _rmsnorm(x, g_pre[...]).astype(bf16)
    u = jnp.dot(h, w_in[...], preferred_element_type=f32) + b_in[...]
    v = u[:, :d] * jax.nn.sigmoid(u[:, d:])

    _fill_ext(v, ext_ref, state_ref, pl.program_id(1) == 0, hist)
    _causal_dwconv(ext_ref, w_dw, b_dw, c_ref, hist)
    _emit_state(ext_ref, state_ref, new_state_ref, hist)

    c = _layernorm(c_ref[...], ln_g[...], ln_b[...])
    m = jnp.dot(jax.nn.silu(c).astype(bf16), w_out[...], preferred_element_type=f32) + b_out[...]
    x = x + _rmsnorm(m, g_post[...])
    y_ref[...] = _swiglu_residual(x, g_fpre[...], w_gu, w_down, g_fpost[...], act_ref)


def _layer_b_kernel(*refs, per_seq_state):
    it = iter(refs)
    x_ref = next(it)
    state_ref = next(it) if per_seq_state else None
    (g_pre, w_in, w_conv, w_out, g_post,
     g_fpre, w_gu, w_down, g_fpost) = [next(it) for _ in range(9)]
    y_ref, new_state_ref = next(it), next(it)
    ext_ref, c_ref, act_ref = next(it), next(it), next(it)

    d = x_ref.shape[-1]
    hist = w_conv.shape[0] - 1
    x = x_ref[...]
    h = _rmsnorm(x, g_pre[...]).astype(bf16)
    u = jnp.dot(h, w_in[...], preferred_element_type=f32)
    bg = u[:, :d]
    z = u[:, d:2 * d] * u[:, 2 * d:]

    _fill_ext(z, ext_ref, state_ref, pl.program_id(1) == 0, hist)
    _causal_dwconv(ext_ref, w_conv, None, c_ref, hist)
    _emit_state(ext_ref, state_ref, new_state_ref, hist)

    m = jnp.dot((bg * c_ref[...]).astype(bf16), w_out[...], preferred_element_type=f32)
    x = x + _rmsnorm(m, g_post[...])
    y_ref[...] = _swiglu_residual(x, g_fpre[...], w_gu, w_down, g_fpost[...], act_ref)


def _resident(arr):
    nd = arr.ndim
    return pl.BlockSpec(arr.shape, lambda b, s: (0,) * nd, pipeline_mode=pl.Buffered(1))


def _run_layer(body, x, state, weights, hist, *, tile, seq_rows, name):
    n_batch, seq, d = x.shape
    d_ff = weights[-2].shape[0]
    n_seq = tile // seq_rows
    pad = _round_up(hist, SUBLANES)
    per_seq_state = state is not None
    assert seq % tile == 0 and tile % seq_rows == 0 and seq_rows % SUBLANES == 0
    assert d % LANES == 0 and (per_seq_state or n_seq == 1)

    x_spec = pl.BlockSpec((None, tile, d), lambda b, s: (b, s, 0))
    in_specs = [x_spec]
    inputs = [x]
    if per_seq_state:
        assert n_batch == 1
        state_spec = pl.BlockSpec((n_seq, hist, d), lambda b, s: (s, 0, 0))
        in_specs.append(state_spec)
        inputs.append(state)
        new_state_shape = state.shape
    else:
        state_spec = pl.BlockSpec((None, hist, d), lambda b, s: (b, 0, 0))
        new_state_shape = (n_batch, hist, d)
    in_specs += [_resident(w) for w in weights]
    inputs += list(weights)

    scratch = [
        pltpu.VMEM((d // LANES, n_seq, pad + seq_rows, LANES), f32),
        pltpu.VMEM((tile, d), f32),
        pltpu.VMEM((tile, d_ff), bf16),
    ]

    return pl.pallas_call(
        functools.partial(body, per_seq_state=per_seq_state),
        grid=(n_batch, seq // tile),
        in_specs=in_specs,
        out_specs=[x_spec, state_spec],
        out_shape=[jax.ShapeDtypeStruct(x.shape, x.dtype),
                   jax.ShapeDtypeStruct(new_state_shape, x.dtype)],
        scratch_shapes=scratch,
        compiler_params=pltpu.CompilerParams(
            dimension_semantics=("arbitrary", "arbitrary"),
            vmem_limit_bytes=VMEM_LIMIT_BYTES),
        name=name,
    )(*inputs)


def kernel(x_prompt, x_sample, state_conv_a, state_conv_b, g_mix_pre, g_mix_post, g_ffn_pre, g_ffn_post, a_w_in, a_b_in, a_w_dw, a_b_dw, a_ln_g, a_ln_b, a_w_out, a_b_out, b_w_in, b_w_conv, b_w_out, w_gate_up, w_down):
    depth = g_mix_pre.shape[0]
    dec_batch, dec_seq, d = x_sample.shape
    row = lambda vec: vec.reshape(1, -1)

    xp = x_prompt
    xs = x_sample.reshape(1, dec_batch * dec_seq, d)
    new_a_p, new_b_p, new_a_s, new_b_s = [], [], [], []
    for i in range(depth):
        j = i // 2
        ffn = (row(g_ffn_pre[i]), w_gate_up[i].astype(bf16), w_down[i].astype(bf16),
               row(g_ffn_post[i]))
        if i % 2 == 0:
            body = _layer_a_kernel
            weights = (row(g_mix_pre[i]), a_w_in[j].astype(bf16), row(a_b_in[j]),
                       a_w_dw[j], row(a_b_dw[j]), row(a_ln_g[j]), row(a_ln_b[j]),
                       a_w_out[j].astype(bf16), row(a_b_out[j]), row(g_mix_post[i])) + ffn
            state, hist = state_conv_a[j], a_w_dw.shape[1] - 1
            outs_p, outs_s = new_a_p, new_a_s
        else:
            body = _layer_b_kernel
            weights = (row(g_mix_pre[i]), b_w_in[j].astype(bf16), b_w_conv[j],
                       b_w_out[j].astype(bf16), row(g_mix_post[i])) + ffn
            state, hist = state_conv_b[j], b_w_conv.shape[1] - 1
            outs_p, outs_s = new_b_p, new_b_s
        xp, st_p = _run_layer(body, xp, None, weights, hist, tile=PROMPT_TILE,
                              seq_rows=PROMPT_TILE, name=f"layer{i}_prompt")
        xs, st_s = _run_layer(body, xs, state, weights, hist, tile=SAMPLE_SEQS * dec_seq,
                              seq_rows=dec_seq, name=f"layer{i}_sample")
        outs_p.append(st_p)
        outs_s.append(st_s)

    return (xp, xs.reshape(x_sample.shape), jnp.stack(new_a_p), jnp.stack(new_b_p),
            jnp.stack(new_a_s), jnp.stack(new_b_s))
```

```python
import functools

import jax
import jax.numpy as jnp
from jax import lax
from jax.experimental import pallas as pl
from jax.experimental.pallas import tpu as pltpu

RMS_EPS = 1e-6
LN_EPS = 1e-5

LANES = 128
SUBLANES = 8
MXU_COLS = 256
VMEM_LIMIT_BYTES = 56 * 1024 * 1024

PROMPT_TILE = 256
SAMPLE_SEQS = 32
CONV_ROWS = 64

f32 = jnp.float32
bf16 = jnp.bfloat16


def _round_up(n, m):
    return -(-n // m) * m


def _lanes(j):
    return slice(j * LANES, (j + 1) * LANES)


def _rmsnorm(x, g):
    ms = jnp.mean(x * x, axis=-1, keepdims=True)
    return x * lax.rsqrt(ms + RMS_EPS) * g


def _layernorm(x, g, b):
    mu = jnp.mean(x, axis=-1, keepdims=True)
    xc = x - mu
    var = jnp.mean(xc * xc, axis=-1, keepdims=True)
    return xc * lax.rsqrt(var + LN_EPS) * g + b


def _ffn_chunks(d_ff):
    step = 3 * MXU_COLS
    return [(a, min(a + step, d_ff)) for a in range(0, d_ff, step)]


def _swiglu_residual(x, g_pre, w_gu_ref, w_down_ref, g_post, act_ref):
    d_ff = w_down_ref.shape[0]
    h = _rmsnorm(x, g_pre).astype(bf16)
    for a, b in _ffn_chunks(d_ff):
        gate = jnp.dot(h, w_gu_ref[:, a:b], preferred_element_type=f32)
        up = jnp.dot(h, w_gu_ref[:, d_ff + a:d_ff + b], preferred_element_type=f32)
        act_ref[:, a:b] = (jax.nn.silu(gate) * up).astype(bf16)
    f = jnp.dot(act_ref[...], w_down_ref[...], preferred_element_type=f32)
    return x + _rmsnorm(f, g_post)


def _fill_ext(v, ext_ref, state_ref, keep_history, hist):
    n_lane_blocks, n_seq, rows, _ = ext_ref.shape
    pad = _round_up(hist, SUBLANES)
    v3 = v.reshape(n_seq, rows - pad, v.shape[-1])
    for j in range(n_lane_blocks):
        if state_ref is not None:
            ext_ref[j, :, pad - hist:pad, :] = state_ref[:, :, _lanes(j)]
        else:
            tail = ext_ref[j, :, rows - pad:, :]
            ext_ref[j, :, 0:pad, :] = jnp.where(keep_history, tail, 0.0)
        ext_ref[j, :, pad:, :] = v3[:, :, _lanes(j)]


def _causal_dwconv(ext_ref, w_ref, b_row, out_ref, hist):
    n_lane_blocks, n_seq, rows, _ = ext_ref.shape
    pad = _round_up(hist, SUBLANES)
    new_rows = rows - pad
    base = pad - hist
    block = min(CONV_ROWS, new_rows)

    def conv_block(g, out_row0, r0, j):
        ls = _lanes(j)
        acc = ext_ref[j, g, base + r0:base + r0 + block, :] * w_ref[0:1, ls]
        for k in range(1, hist + 1):
            acc = acc + ext_ref[j, g, base + r0 + k:base + r0 + k + block, :] * w_ref[k:k + 1, ls]
        if b_row is not None:
            acc = acc + b_row[:, ls]
        out_ref[pl.ds(out_row0, block), ls] = acc

    if n_seq == 1:
        for r0 in range(0, new_rows, block):
            for j in range(n_lane_blocks):
                conv_block(0, r0, r0, j)
    else:
        def body(g, carry):
            for j in range(n_lane_blocks):
                conv_block(g, pl.multiple_of(g * new_rows, SUBLANES), 0, j)
            return carry
        lax.fori_loop(0, n_seq, body, 0)


def _layer_kernel(*refs, mixer, per_seq_state, has_prev, steps_per_seq, n_tiles):
    it = iter(refs)
    x_ref = next(it)
    state_ref = next(it) if per_seq_state else None
    if has_prev:
        next(it)
    vec = next(it)
    avec = next(it) if mixer == "a" else None
    w_in, w_conv, w_out, w_gu, w_down = [next(it) for _ in range(5)]
    y_ref, new_state_ref = next(it), next(it)
    ext_ref, c_ref, act_ref = next(it), next(it), next(it)

    step = pl.program_id(0)
    d = x_ref.shape[-1]
    hist = w_conv.shape[0] - 1
    n_lane_blocks, _, rows, _ = ext_ref.shape

    @pl.when(step < n_tiles)
    def _():
        if not per_seq_state:
            @pl.when(step == 0)
            def _():
                ext_ref[...] = jnp.zeros(ext_ref.shape, f32)

        x = x_ref[...]
        h = _rmsnorm(x, vec[0:1, :]).astype(bf16)
        u = jnp.dot(h, w_in[...], preferred_element_type=f32)
        keep_history = step % steps_per_seq != 0
        if mixer == "a":
            v = (u[:, :d] + avec[0:1, :]) * jax.nn.sigmoid(u[:, d:] + avec[1:2, :])
            _fill_ext(v, ext_ref, state_ref, keep_history, hist)
            _causal_dwconv(ext_ref, w_conv, avec[2:3, :], c_ref, hist)
            c = _layernorm(c_ref[...], avec[3:4, :], avec[4:5, :])
            m = jnp.dot(jax.nn.silu(c).astype(bf16), w_out[...], preferred_element_type=f32)
            m = m + avec[5:6, :]
        else:
            _fill_ext(u[:, d:2 * d] * u[:, 2 * d:], ext_ref, state_ref, keep_history, hist)
            _causal_dwconv(ext_ref, w_conv, None, c_ref, hist)
            m = jnp.dot((u[:, :d] * c_ref[...]).astype(bf16), w_out[...],
                        preferred_element_type=f32)
        x = x + _rmsnorm(m, vec[1:2, :])
        y_ref[...] = _swiglu_residual(x, vec[2:3, :], w_gu, w_down, vec[3:4, :], act_ref)

        for j in range(n_lane_blocks):
            tail = ext_ref[j, :, rows - hist:, :]
            if per_seq_state:
                new_state_ref[:, :, _lanes(j)] = tail
            else:
                new_state_ref[:, _lanes(j)] = tail[0]

    @pl.when(step >= n_tiles)
    def _():
        new_state_ref[...] = jnp.zeros(new_state_ref.shape, f32)


def _layer_weights(arr, idx):
    nd = arr.ndim
    return pl.BlockSpec((None,) + arr.shape[1:], lambda n: (idx,) + (0,) * (nd - 1),
                        pipeline_mode=pl.Buffered(1))


def _run_layer(x, state_all, prev_state, vecs, avecs, w_in, w_conv, w_out, w_gu, w_down,
               *, mixer, layer, tile, seq_rows, name):
    n_batch, seq, d = x.shape
    j = layer // 2
    hist = w_conv.shape[1] - 1
    n_slabs = w_conv.shape[0]
    d_ff = w_down.shape[1]
    n_seq = tile // seq_rows
    pad = _round_up(hist, SUBLANES)
    per_seq_state = state_all is not None
    steps_per_seq = seq // tile
    n_tiles = n_batch * steps_per_seq
    assert seq % tile == 0 and tile % seq_rows == 0 and seq_rows % SUBLANES == 0
    assert d % LANES == 0 and (per_seq_state or (n_seq == 1 and tile >= pad))
    assert not (per_seq_state and n_batch != 1)

    last = n_tiles - 1
    x_spec = pl.BlockSpec(
        (None, tile, d),
        lambda n: (jnp.minimum(n, last) // steps_per_seq, jnp.minimum(n, last) % steps_per_seq, 0))

    in_specs, inputs = [x_spec], [x]
    if per_seq_state:
        state_block = (n_seq, hist, d)
        n_state_blocks = n_tiles
        block_of_tile = lambda t: t
        in_specs.append(pl.BlockSpec((None,) + state_block,
                                     lambda n: (j, jnp.minimum(n, last), 0, 0)))
        inputs.append(state_all)
        n_state_rows = state_all.shape[1]
    else:
        state_block = (None, hist, d)
        n_state_blocks = n_batch
        block_of_tile = lambda t: t // steps_per_seq
        n_state_rows = n_batch

    aliases = {}
    if prev_state is None:
        n_fill = (n_slabs - 1) * n_state_blocks
        other_slabs = [s for s in range(n_slabs) if s != j]
        assert other_slabs == list(range(j + 1, n_slabs))
    else:
        n_fill = 0
        aliases[len(inputs)] = 1
        in_specs.append(pl.BlockSpec(memory_space=pl.ANY))
        inputs.append(prev_state)

    def new_state_index(n):
        f = jnp.maximum(n - n_tiles, 0)
        slab = jnp.where(n < n_tiles, j, j + 1 + f // n_state_blocks)
        block = jnp.where(n < n_tiles, block_of_tile(jnp.minimum(n, last)), f % n_state_blocks)
        return (slab, block, 0, 0)

    new_state_spec = pl.BlockSpec((None,) + state_block, new_state_index)

    in_specs.append(_layer_weights(vecs, layer))
    inputs.append(vecs)
    if mixer == "a":
        in_specs.append(_layer_weights(avecs, j))
        inputs.append(avecs)
    in_specs += [_layer_weights(w_in, j), _layer_weights(w_conv, j), _layer_weights(w_out, j),
                 _layer_weights(w_gu, layer), _layer_weights(w_down, layer)]
    inputs += [w_in, w_conv, w_out, w_gu, w_down]

    scratch = [
        pltpu.VMEM((d // LANES, n_seq, pad + seq_rows, LANES), f32),
        pltpu.VMEM((tile, d), f32),
        pltpu.VMEM((tile, d_ff), bf16),
    ]

    body = functools.partial(_layer_kernel, mixer=mixer, per_seq_state=per_seq_state,
                             has_prev=prev_state is not None,
                             steps_per_seq=steps_per_seq, n_tiles=n_tiles)
    return pl.pallas_call(
        body,
        grid=(n_tiles + n_fill,),
        in_specs=in_specs,
        out_specs=[x_spec, new_state_spec],
        out_shape=[jax.ShapeDtypeStruct(x.shape, x.dtype),
                   jax.ShapeDtypeStruct((n_slabs, n_state_rows, hist, d), x.dtype)],
        input_output_aliases=aliases,
        scratch_shapes=scratch,
        compiler_params=pltpu.CompilerParams(
            dimension_semantics=("arbitrary",),
            vmem_limit_bytes=VMEM_LIMIT_BYTES),
        name=name,
    )(*inputs)


def kernel(x_prompt, x_sample, state_conv_a, state_conv_b, g_mix_pre, g_mix_post, g_ffn_pre, g_ffn_post, a_w_in, a_b_in, a_w_dw, a_b_dw, a_ln_g, a_ln_b, a_w_out, a_b_out, b_w_in, b_w_conv, b_w_out, w_gate_up, w_down):
    depth, d = g_mix_pre.shape
    dec_batch, dec_seq, _ = x_sample.shape
    n_a = a_w_in.shape[0]

    vecs = jnp.stack([g_mix_pre, g_mix_post, g_ffn_pre, g_ffn_post], axis=1)
    avecs = jnp.concatenate([a_b_in.reshape(n_a, 2, d), a_b_dw[:, None], a_ln_g[:, None],
                             a_ln_b[:, None], a_b_out[:, None]], axis=1)
    w_gu, w_dn = w_gate_up.astype(bf16), w_down.astype(bf16)
    mix = {"a": (a_w_in.astype(bf16), a_w_dw, a_w_out.astype(bf16)),
           "b": (b_w_in.astype(bf16), b_w_conv, b_w_out.astype(bf16))}
    states = {"a": state_conv_a, "b": state_conv_b}

    xp = x_prompt
    xs = x_sample.reshape(1, dec_batch * dec_seq, d)
    new_p = {"a": None, "b": None}
    new_s = {"a": None, "b": None}
    for i in range(depth):
        m = "ab"[i % 2]
        w_in, w_conv, w_out = mix[m]
        common = dict(mixer=m, layer=i)
        xp, new_p[m] = _run_layer(xp, None, new_p[m], vecs, avecs, w_in, w_conv, w_out, w_gu, w_dn,
                                  tile=PROMPT_TILE, seq_rows=PROMPT_TILE,
                                  name=f"layer{i}_prompt", **common)
        xs, new_s[m] = _run_layer(xs, states[m], new_s[m], vecs, avecs, w_in, w_conv, w_out, w_gu, w_dn,
                                  tile=SAMPLE_SEQS * dec_seq, seq_rows=dec_seq,
                                  name=f"layer{i}_sample", **common)

    return (xp, xs.reshape(x_sample.shape), new_p["a"], new_p["b"], new_s["a"], new_s["b"])
```

```python
import functools

import jax
import jax.numpy as jnp
from jax import lax
from jax.experimental import pallas as pl
from jax.experimental.pallas import tpu as pltpu

RMS_EPS = 1e-6
LN_EPS = 1e-5

LANES = 128
SUBLANES = 8
MXU_COLS = 256
VMEM_LIMIT_BYTES = 56 * 1024 * 1024

PROMPT_TILE = 512
SAMPLE_SEQS = 32
CONV_ROWS = 64
PIPELINE_LAG = 2

f32 = jnp.float32
bf16 = jnp.bfloat16


def _round_up(n, m):
    return -(-n // m) * m


def _lanes(j):
    return slice(j * LANES, (j + 1) * LANES)


def _rmsnorm(x, g):
    ms = jnp.mean(x * x, axis=-1, keepdims=True)
    return x * lax.rsqrt(ms + RMS_EPS) * g


def _layernorm(x, g, b):
    mu = jnp.mean(x, axis=-1, keepdims=True)
    xc = x - mu
    var = jnp.mean(xc * xc, axis=-1, keepdims=True)
    return xc * lax.rsqrt(var + LN_EPS) * g + b


def _interleave_gate_up(w_gate_up):
    *lead, d, two_f = w_gate_up.shape
    n_blocks = two_f // (2 * MXU_COLS)
    w = w_gate_up.reshape(*lead, d, 2, n_blocks, MXU_COLS)
    return jnp.swapaxes(w, -3, -2).reshape(*lead, d, two_f)


def _swiglu_residual(x, g_pre, w_gu_ref, w_down_ref, g_post, act_ref):
    d_ff = w_down_ref.shape[0]
    h = _rmsnorm(x, g_pre).astype(bf16)
    for c in range(d_ff // MXU_COLS):
        gu = jnp.dot(h, w_gu_ref[:, 2 * c * MXU_COLS:2 * (c + 1) * MXU_COLS],
                     preferred_element_type=f32)
        act = jax.nn.silu(gu[:, :MXU_COLS]) * gu[:, MXU_COLS:]
        act_ref[:, c * MXU_COLS:(c + 1) * MXU_COLS] = act.astype(bf16)
    f = jnp.dot(act_ref[...], w_down_ref[...], preferred_element_type=f32)
    return x + _rmsnorm(f, g_post)


def _causal_dwconv(ext_ref, w_ref, b_row, out_ref, hist):
    n_lane_blocks, n_seq, rows, _ = ext_ref.shape
    pad = _round_up(hist, SUBLANES)
    new_rows = rows - pad
    base = pad - hist
    block = min(CONV_ROWS, new_rows)

    def conv_block(g, out_row0, r0, j):
        ls = _lanes(j)
        acc = ext_ref[j, g, base + r0:base + r0 + block, :] * w_ref[0:1, ls]
        for k in range(1, hist + 1):
            acc = acc + ext_ref[j, g, base + r0 + k:base + r0 + k + block, :] * w_ref[k:k + 1, ls]
        if b_row is not None:
            acc = acc + b_row[:, ls]
        out_ref[pl.ds(out_row0, block), ls] = acc

    if n_seq == 1:
        for r0 in range(0, new_rows, block):
            for j in range(n_lane_blocks):
                conv_block(0, r0, r0, j)
    else:
        def body(g, carry):
            for j in range(n_lane_blocks):
                conv_block(g, pl.multiple_of(g * new_rows, SUBLANES), 0, j)
            return carry
        lax.fori_loop(0, n_seq, body, 0)


def _layer_kernel(*refs, mixer, per_seq_state, has_prev, pipelined, steps_per_seq, n_tiles):
    it = iter(refs)
    xa_ref = next(it)
    xb_ref = next(it) if pipelined else xa_ref
    state_ref = next(it) if per_seq_state else None
    if has_prev:
        next(it)
    vec = next(it)
    avec = next(it) if mixer == "a" else None
    w_in, w_conv, w_out, w_gu, w_down = [next(it) for _ in range(5)]
    y_ref, new_state_ref = next(it), next(it)
    ext_ref, c_ref, act_ref = next(it), next(it), next(it)
    gate_ref = next(it) if mixer == "b" else None
    xmid_ref = next(it) if pipelined else None

    step = pl.program_id(0)
    d = xa_ref.shape[-1]
    hist = w_conv.shape[0] - 1
    _, n_lane_blocks, n_seq, rows, _ = ext_ref.shape
    pad = _round_up(hist, SUBLANES)
    n_main = n_tiles + (PIPELINE_LAG if pipelined else 0)

    slot_a = step % 2 if pipelined else 0
    slot_b = 1 - slot_a if pipelined else 0

    def stage_a():
        x = xa_ref[...]
        h = _rmsnorm(x, vec[0:1, :]).astype(bf16)
        u = jnp.dot(h, w_in[...], preferred_element_type=f32)
        if mixer == "a":
            v = (u[:, :d] + avec[0:1, :]) * jax.nn.sigmoid(u[:, d:] + avec[1:2, :])
        else:
            gate_ref[slot_a] = u[:, :d]
            v = u[:, d:2 * d] * u[:, 2 * d:]
        v3 = v.reshape(n_seq, rows - pad, d)
        keep_history = step % steps_per_seq != 0
        for j in range(n_lane_blocks):
            if per_seq_state:
                ext_ref[slot_a, j, :, pad - hist:pad, :] = state_ref[:, :, _lanes(j)]
            else:
                tail = ext_ref[slot_b, j, :, rows - pad:, :]
                ext_ref[slot_a, j, :, 0:pad, :] = jnp.where(keep_history, tail, 0.0)
            ext_ref[slot_a, j, :, pad:, :] = v3[:, :, _lanes(j)]

    def stage_b():
        ext = ext_ref.at[slot_b]
        if mixer == "a":
            _causal_dwconv(ext, w_conv, avec[2:3, :], c_ref, hist)
            c = _layernorm(c_ref[...], avec[3:4, :], avec[4:5, :])
            m = jnp.dot(jax.nn.silu(c).astype(bf16), w_out[...], preferred_element_type=f32)
            m = m + avec[5:6, :]
        else:
            _causal_dwconv(ext, w_conv, None, c_ref, hist)
            m = jnp.dot((gate_ref[slot_b] * c_ref[...]).astype(bf16), w_out[...],
                        preferred_element_type=f32)
        return xb_ref[...] + _rmsnorm(m, vec[1:2, :])

    def stage_c(x_mid):
        return _swiglu_residual(x_mid, vec[2:3, :], w_gu, w_down, vec[3:4, :], act_ref)

    @pl.when(step < n_main)
    def _():
        if not per_seq_state:
            @pl.when(step == 0)
            def _():
                ext_ref[...] = jnp.zeros(ext_ref.shape, f32)
                if pipelined:
                    xmid_ref[...] = jnp.zeros(xmid_ref.shape, f32)
                    if gate_ref is not None:
                        gate_ref[...] = jnp.zeros(gate_ref.shape, f32)

        if pipelined:
            y_ref[...] = stage_c(xmid_ref[slot_a])
            xmid_ref[slot_b] = stage_b()
            stage_a()
        else:
            stage_a()
            y_ref[...] = stage_c(stage_b())

        @pl.when(step < n_tiles)
        def _():
            for j in range(n_lane_blocks):
                tail = ext_ref[slot_a, j, :, rows - hist:, :]
                if per_seq_state:
                    new_state_ref[:, :, _lanes(j)] = tail
                else:
                    new_state_ref[:, _lanes(j)] = tail[0]

    @pl.when(step >= n_main)
    def _():
        new_state_ref[...] = jnp.zeros(new_state_ref.shape, f32)


def _layer_weights(arr, idx):
    nd = arr.ndim
    return pl.BlockSpec((None,) + arr.shape[1:], lambda n: (idx,) + (0,) * (nd - 1),
                        pipeline_mode=pl.Buffered(1))


def _run_layer(x, state_all, prev_state, vecs, avecs, w_in, w_conv, w_out, w_gu, w_down,
               *, mixer, layer, tile, seq_rows, pipelined, name):
    n_batch, seq, d = x.shape
    j = layer // 2
    hist = w_conv.shape[1] - 1
    n_slabs = w_conv.shape[0]
    d_ff = w_down.shape[1]
    n_seq = tile // seq_rows
    pad = _round_up(hist, SUBLANES)
    per_seq_state = state_all is not None
    steps_per_seq = seq // tile
    n_tiles = n_batch * steps_per_seq
    n_main = n_tiles + (PIPELINE_LAG if pipelined else 0)
    assert seq % tile == 0 and tile % seq_rows == 0 and seq_rows % SUBLANES == 0
    assert d % LANES == 0 and d_ff % MXU_COLS == 0
    assert per_seq_state or (n_seq == 1 and tile >= pad)
    assert not (per_seq_state and (n_batch != 1 or pipelined))

    last = n_tiles - 1

    def tile_spec(lag):
        def index(n):
            t = jnp.clip(n - lag, 0, last)
            return (t // steps_per_seq, t % steps_per_seq, 0)
        return pl.BlockSpec((None, tile, d), index)

    in_specs, inputs = [tile_spec(0)], [x]
    if pipelined:
        in_specs.append(tile_spec(1))
        inputs.append(x)
    if per_seq_state:
        state_block = (n_seq, hist, d)
        n_state_blocks = n_tiles
        block_of_tile = lambda t: t
        in_specs.append(pl.BlockSpec((None,) + state_block,
                                     lambda n: (j, jnp.minimum(n, last), 0, 0)))
        inputs.append(state_all)
        n_state_rows = state_all.shape[1]
    else:
        state_block = (None, hist, d)
        n_state_blocks = n_batch
        block_of_tile = lambda t: t // steps_per_seq
        n_state_rows = n_batch

    aliases = {}
    if prev_state is None:
        assert j == 0
        n_fill = (n_slabs - 1) * n_state_blocks
    else:
        n_fill = 0
        aliases[len(inputs)] = 1
        in_specs.append(pl.BlockSpec(memory_space=pl.ANY))
        inputs.append(prev_state)

    def new_state_index(n):
        f = jnp.maximum(n - n_main, 0)
        slab = jnp.where(n < n_main, j, j + 1 + f // n_state_blocks)
        block = jnp.where(n < n_main, block_of_tile(jnp.minimum(n, last)), f % n_state_blocks)
        return (slab, block, 0, 0)

    new_state_spec = pl.BlockSpec((None,) + state_block, new_state_index)

    in_specs.append(_layer_weights(vecs, layer))
    inputs.append(vecs)
    if mixer == "a":
        in_specs.append(_layer_weights(avecs, j))
        inputs.append(avecs)
    in_specs += [_layer_weights(w_in, j), _layer_weights(w_conv, j), _layer_weights(w_out, j),
                 _layer_weights(w_gu, layer), _layer_weights(w_down, layer)]
    inputs += [w_in, w_conv, w_out, w_gu, w_down]

    n_buf = 2 if pipelined else 1
    scratch = [
        pltpu.VMEM((n_buf, d // LANES, n_seq, pad + seq_rows, LANES), f32),
        pltpu.VMEM((tile, d), f32),
        pltpu.VMEM((tile, d_ff), bf16),
    ]
    if mixer == "b":
        scratch.append(pltpu.VMEM((n_buf, tile, d), f32))
    if pipelined:
        scratch.append(pltpu.VMEM((2, tile, d), f32))

    body = functools.partial(_layer_kernel, mixer=mixer, per_seq_state=per_seq_state,
                             has_prev=prev_state is not None, pipelined=pipelined,
                             steps_per_seq=steps_per_seq, n_tiles=n_tiles)
    return pl.pallas_call(
        body,
        grid=(n_main + n_fill,),
        in_specs=in_specs,
        out_specs=[tile_spec(PIPELINE_LAG if pipelined else 0), new_state_spec],
        out_shape=[jax.ShapeDtypeStruct(x.shape, x.dtype),
                   jax.ShapeDtypeStruct((n_slabs, n_state_rows, hist, d), x.dtype)],
        input_output_aliases=aliases,
        scratch_shapes=scratch,
        compiler_params=pltpu.CompilerParams(
            dimension_semantics=("arbitrary",),
            vmem_limit_bytes=VMEM_LIMIT_BYTES),
        name=name,
    )(*inputs)


def kernel(x_prompt, x_sample, state_conv_a, state_conv_b, g_mix_pre, g_mix_post, g_ffn_pre, g_ffn_post, a_w_in, a_b_in, a_w_dw, a_b_dw, a_ln_g, a_ln_b, a_w_out, a_b_out, b_w_in, b_w_conv, b_w_out, w_gate_up, w_down):
    depth, d = g_mix_pre.shape
    dec_batch, dec_seq, _ = x_sample.shape
    n_a = a_w_in.shape[0]

    vecs = jnp.stack([g_mix_pre, g_mix_post, g_ffn_pre, g_ffn_post], axis=1)
    avecs = jnp.concatenate([a_b_in.reshape(n_a, 2, d), a_b_dw[:, None], a_ln_g[:, None],
                             a_ln_b[:, None], a_b_out[:, None]], axis=1)
    w_gu, w_dn = _interleave_gate_up(w_gate_up).astype(bf16), w_down.astype(bf16)
    mix = {"a": (a_w_in.astype(bf16), a_w_dw, a_w_out.astype(bf16)),
           "b": (b_w_in.astype(bf16), b_w_conv, b_w_out.astype(bf16))}
    states = {"a": state_conv_a, "b": state_conv_b}

    xp = x_prompt
    xs = x_sample.reshape(1, dec_batch * dec_seq, d)
    new_p = {"a": None, "b": None}
    new_s = {"a": None, "b": None}
    for i in range(depth):
        m = "ab"[i % 2]
        w_in, w_conv, w_out = mix[m]
        common = dict(mixer=m, layer=i)
        xp, new_p[m] = _run_layer(xp, None, new_p[m], vecs, avecs, w_in, w_conv, w_out, w_gu, w_dn,
                                  tile=PROMPT_TILE, seq_rows=PROMPT_TILE, pipelined=False,
                                  name=f"layer{i}_prompt", **common)
        xs, new_s[m] = _run_layer(xs, states[m], new_s[m], vecs, avecs, w_in, w_conv, w_out, w_gu, w_dn,
                                  tile=SAMPLE_SEQS * dec_seq, seq_rows=dec_seq, pipelined=False,
                                  name=f"layer{i}_sample", **common)

    return (xp, xs.reshape(x_sample.shape), new_p["a"], new_p["b"], new_s["a"], new_s["b"])
```

```python
import functools

import jax
import jax.numpy as jnp
from jax import lax
from jax.experimental import pallas as pl
from jax.experimental.pallas import tpu as pltpu

RMS_EPS = 1e-6
LN_EPS = 1e-5

LANES = 128
SUBLANES = 8
MXU_COLS = 256
VMEM_LIMIT_BYTES = 56 * 1024 * 1024

PROMPT_TILE = 512
SAMPLE_SEQS = 32
CONV_ROWS = 64

f32 = jnp.float32
bf16 = jnp.bfloat16


def _round_up(n, m):
    return -(-n // m) * m


def _lanes(j):
    return slice(j * LANES, (j + 1) * LANES)


def _rmsnorm(x, g):
    ms = jnp.mean(x * x, axis=-1, keepdims=True)
    return x * lax.rsqrt(ms + RMS_EPS) * g


def _layernorm(x, g, b):
    mu = jnp.mean(x, axis=-1, keepdims=True)
    xc = x - mu
    var = jnp.mean(xc * xc, axis=-1, keepdims=True)
    return xc * lax.rsqrt(var + LN_EPS) * g + b


def _ffn_chunks(d_ff):
    step = 3 * MXU_COLS
    return [(a, min(a + step, d_ff)) for a in range(0, d_ff, step)]


def _swiglu_residual(x, g_pre, w_gu_ref, w_down_ref, g_post, act_ref):
    d_ff = w_down_ref.shape[0]
    h = _rmsnorm(x, g_pre).astype(bf16)
    for a, b in _ffn_chunks(d_ff):
        gate = jnp.dot(h, w_gu_ref[:, a:b], preferred_element_type=f32)
        up = jnp.dot(h, w_gu_ref[:, d_ff + a:d_ff + b], preferred_element_type=f32)
        act_ref[:, a:b] = (jax.nn.silu(gate) * up).astype(bf16)
    f = jnp.dot(act_ref[...], w_down_ref[...], preferred_element_type=f32)
    return x + _rmsnorm(f, g_post)


def _fill_ext(v, ext_ref, state_ref, keep_history, hist):
    n_lane_blocks, n_seq, rows, _ = ext_ref.shape
    pad = _round_up(hist, SUBLANES)
    v3 = v.reshape(n_seq, rows - pad, v.shape[-1])
    for j in range(n_lane_blocks):
        if state_ref is not None:
            ext_ref[j, :, pad - hist:pad, :] = state_ref[:, :, _lanes(j)]
        else:
            tail = ext_ref[j, :, rows - pad:, :]
            ext_ref[j, :, 0:pad, :] = jnp.where(keep_history, tail, 0.0)
        ext_ref[j, :, pad:, :] = v3[:, :, _lanes(j)]


def _causal_dwconv(ext_ref, w_ref, b_row, out_ref, hist):
    n_lane_blocks, n_seq, rows, _ = ext_ref.shape
    pad = _round_up(hist, SUBLANES)
    new_rows = rows - pad
    base = pad - hist
    block = min(CONV_ROWS, new_rows)

    def conv_block(g, out_row0, r0, j):
        ls = _lanes(j)
        acc = ext_ref[j, g, base + r0:base + r0 + block, :] * w_ref[0:1, ls]
        for k in range(1, hist + 1):
            acc = acc + ext_ref[j, g, base + r0 + k:base + r0 + k + block, :] * w_ref[k:k + 1, ls]
        if b_row is not None:
            acc = acc + b_row[:, ls]
        out_ref[pl.ds(out_row0, block), ls] = acc

    if n_seq == 1:
        for r0 in range(0, new_rows, block):
            for j in range(n_lane_blocks):
                conv_block(0, r0, r0, j)
    else:
        def body(g, carry):
            for j in range(n_lane_blocks):
                conv_block(g, pl.multiple_of(g * new_rows, SUBLANES), 0, j)
            return carry
        lax.fori_loop(0, n_seq, body, 0)


def _layer_kernel(*refs, mixer, per_seq_state, has_prev, steps_per_seq, n_tiles):
    it = iter(refs)
    x_ref = next(it)
    state_ref = next(it) if per_seq_state else None
    if has_prev:
        next(it)
    vec = next(it)
    avec = next(it) if mixer == "a" else None
    w_in, w_conv, w_out, w_gu, w_down = [next(it) for _ in range(5)]
    y_ref, new_state_ref = next(it), next(it)
    ext_ref, c_ref, act_ref = next(it), next(it), next(it)

    step = pl.program_id(0)
    d = x_ref.shape[-1]
    hist = w_conv.shape[0] - 1
    n_lane_blocks, _, rows, _ = ext_ref.shape

    @pl.when(step < n_tiles)
    def _():
        if not per_seq_state:
            @pl.when(step == 0)
            def _():
                ext_ref[...] = jnp.zeros(ext_ref.shape, f32)

        x = x_ref[...]
        h = _rmsnorm(x, vec[0:1, :]).astype(bf16)
        u = jnp.dot(h, w_in[...], preferred_element_type=f32)
        keep_history = step % steps_per_seq != 0
        if mixer == "a":
            v = (u[:, :d] + avec[0:1, :]) * jax.nn.sigmoid(u[:, d:] + avec[1:2, :])
            _fill_ext(v, ext_ref, state_ref, keep_history, hist)
            _causal_dwconv(ext_ref, w_conv, avec[2:3, :], c_ref, hist)
            c = _layernorm(c_ref[...], avec[3:4, :], avec[4:5, :])
            m = jnp.dot(jax.nn.silu(c).astype(bf16), w_out[...], preferred_element_type=f32)
            m = m + avec[5:6, :]
        else:
            _fill_ext(u[:, d:2 * d] * u[:, 2 * d:], ext_ref, state_ref, keep_history, hist)
            _causal_dwconv(ext_ref, w_conv, None, c_ref, hist)
            m = jnp.dot((u[:, :d] * c_ref[...]).astype(bf16), w_out[...],
                        preferred_element_type=f32)
        x = x + _rmsnorm(m, vec[1:2, :])
        y_ref[...] = _swiglu_residual(x, vec[2:3, :], w_gu, w_down, vec[3:4, :], act_ref)

        for j in range(n_lane_blocks):
            tail = ext_ref[j, :, rows - hist:, :]
            if per_seq_state:
                new_state_ref[:, :, _lanes(j)] = tail
            else:
                new_state_ref[:, _lanes(j)] = tail[0]

    @pl.when(step >= n_tiles)
    def _():
        new_state_ref[...] = jnp.zeros(new_state_ref.shape, f32)


def _layer_weights(arr, idx):
    nd = arr.ndim
    return pl.BlockSpec((None,) + arr.shape[1:], lambda n: (idx,) + (0,) * (nd - 1),
                        pipeline_mode=pl.Buffered(1))


def _run_layer(x, state_all, prev_state, vecs, avecs, w_in, w_conv, w_out, w_gu, w_down,
               *, mixer, layer, tile, seq_rows, name):
    n_batch, seq, d = x.shape
    j = layer // 2
    hist = w_conv.shape[1] - 1
    n_slabs = w_conv.shape[0]
    d_ff = w_down.shape[1]
    n_seq = tile // seq_rows
    pad = _round_up(hist, SUBLANES)
    per_seq_state = state_all is not None
    steps_per_seq = seq // tile
    n_tiles = n_batch * steps_per_seq
    assert seq % tile == 0 and tile % seq_rows == 0 and seq_rows % SUBLANES == 0
    assert d % LANES == 0 and (per_seq_state or (n_seq == 1 and tile >= pad))
    assert not (per_seq_state and n_batch != 1)

    last = n_tiles - 1
    x_spec = pl.BlockSpec(
        (None, tile, d),
        lambda n: (jnp.minimum(n, last) // steps_per_seq, jnp.minimum(n, last) % steps_per_seq, 0))

    in_specs, inputs = [x_spec], [x]
    if per_seq_state:
        state_block = (n_seq, hist, d)
        n_state_blocks = n_tiles
        block_of_tile = lambda t: t
        in_specs.append(pl.BlockSpec((None,) + state_block,
                                     lambda n: (j, jnp.minimum(n, last), 0, 0)))
        inputs.append(state_all)
        n_state_rows = state_all.shape[1]
    else:
        state_block = (None, hist, d)
        n_state_blocks = n_batch
        block_of_tile = lambda t: t // steps_per_seq
        n_state_rows = n_batch

    aliases = {}
    if prev_state is None:
        assert j == 0
        n_fill = (n_slabs - 1) * n_state_blocks
    else:
        n_fill = 0
        aliases[len(inputs)] = 1
        in_specs.append(pl.BlockSpec(memory_space=pl.ANY))
        inputs.append(prev_state)

    def new_state_index(n):
        f = jnp.maximum(n - n_tiles, 0)
        slab = jnp.where(n < n_tiles, j, j + 1 + f // n_state_blocks)
        block = jnp.where(n < n_tiles, block_of_tile(jnp.minimum(n, last)), f % n_state_blocks)
        return (slab, block, 0, 0)

    new_state_spec = pl.BlockSpec((None,) + state_block, new_state_index)

    in_specs.append(_layer_weights(vecs, layer))
    inputs.append(vecs)
    if mixer == "a":
        in_specs.append(_layer_weights(avecs, j))
        inputs.append(avecs)
    in_specs += [_layer_weights(w_in, j), _layer_weights(w_conv, j), _layer_weights(w_out, j),
                 _layer_weights(w_gu, layer), _layer_weights(w_down, layer)]
    inputs += [w_in, w_conv, w_out, w_gu, w_down]

    scratch = [
        pltpu.VMEM((d // LANES, n_seq, pad + seq_rows, LANES), f32),
        pltpu.VMEM((tile, d), f32),
        pltpu.VMEM((tile, d_ff), bf16),
    ]

    body = functools.partial(_layer_kernel, mixer=mixer, per_seq_state=per_seq_state,
                             has_prev=prev_state is not None,
                             steps_per_seq=steps_per_seq, n_tiles=n_tiles)
    return pl.pallas_call(
        body,
        grid=(n_tiles + n_fill,),
        in_specs=in_specs,
        out_specs=[x_spec, new_state_spec],
        out_shape=[jax.ShapeDtypeStruct(x.shape, x.dtype),
                   jax.ShapeDtypeStruct((n_slabs, n_state_rows, hist, d), x.dtype)],
        input_output_aliases=aliases,
        scratch_shapes=scratch,
        compiler_params=pltpu.CompilerParams(
            dimension_semantics=("arbitrary",),
            vmem_limit_bytes=VMEM_LIMIT_BYTES),
        name=name,
    )(*inputs)


def kernel(x_prompt, x_sample, state_conv_a, state_conv_b, g_mix_pre, g_mix_post, g_ffn_pre, g_ffn_post, a_w_in, a_b_in, a_w_dw, a_b_dw, a_ln_g, a_ln_b, a_w_out, a_b_out, b_w_in, b_w_conv, b_w_out, w_gate_up, w_down):
    depth, d = g_mix_pre.shape
    dec_batch, dec_seq, _ = x_sample.shape
    n_a = a_w_in.shape[0]

    vecs = jnp.stack([g_mix_pre, g_mix_post, g_ffn_pre, g_ffn_post], axis=1)
    avecs = jnp.concatenate([a_b_in.reshape(n_a, 2, d), a_b_dw[:, None], a_ln_g[:, None],
                             a_ln_b[:, None], a_b_out[:, None]], axis=1)
    w_gu, w_dn = w_gate_up.astype(bf16), w_down.astype(bf16)
    mix = {"a": (a_w_in.astype(bf16), a_w_dw, a_w_out.astype(bf16)),
           "b": (b_w_in.astype(bf16), b_w_conv, b_w_out.astype(bf16))}
    states = {"a": state_conv_a, "b": state_conv_b}

    xp = x_prompt
    xs = x_sample.reshape(1, dec_batch * dec_seq, d)
    new_p = {"a": None, "b": None}
    new_s = {"a": None, "b": None}
    for i in range(depth):
        m = "ab"[i % 2]
        w_in, w_conv, w_out = mix[m]
        common = dict(mixer=m, layer=i)
        xp, new_p[m] = _run_layer(xp, None, new_p[m], vecs, avecs, w_in, w_conv, w_out, w_gu, w_dn,
                                  tile=PROMPT_TILE, seq_rows=PROMPT_TILE,
                                  name=f"layer{i}_prompt", **common)
        xs, new_s[m] = _run_layer(xs, states[m], new_s[m], vecs, avecs, w_in, w_conv, w_out, w_gu, w_dn,
                                  tile=SAMPLE_SEQS * dec_seq, seq_rows=dec_seq,
                                  name=f"layer{i}_sample", **common)

    return (xp, xs.reshape(x_sample.shape), new_p["a"], new_p["b"], new_s["a"], new_s["b"])
```

```python
import functools

import jax
import jax.numpy as jnp
from jax import lax
from jax.experimental import pallas as pl
from jax.experimental.pallas import tpu as pltpu

RMS_EPS = 1e-6
LN_EPS = 1e-5

LANES = 128
SUBLANES = 8
MXU_COLS = 256
VMEM_LIMIT_BYTES = 56 * 1024 * 1024

PROMPT_TILE = 512
SAMPLE_SEQS = {"a": 32, "b": 64}
CONV_ROWS = 64

f32 = jnp.float32
bf16 = jnp.bfloat16


def _round_up(n, m):
    return -(-n // m) * m


def _lanes(j):
    return slice(j * LANES, (j + 1) * LANES)


def _rmsnorm(x, g):
    ms = jnp.mean(x * x, axis=-1, keepdims=True)
    return x * lax.rsqrt(ms + RMS_EPS) * g


def _layernorm(x, g, b):
    mu = jnp.mean(x, axis=-1, keepdims=True)
    xc = x - mu
    var = jnp.mean(xc * xc, axis=-1, keepdims=True)
    return xc * lax.rsqrt(var + LN_EPS) * g + b


def _ffn_chunks(d_ff):
    step = 3 * MXU_COLS
    return [(a, min(a + step, d_ff)) for a in range(0, d_ff, step)]


def _swiglu_residual(x, g_pre, w_gu_ref, w_down_ref, g_post, act_ref):
    d_ff = w_down_ref.shape[0]
    h = _rmsnorm(x, g_pre).astype(bf16)
    for a, b in _ffn_chunks(d_ff):
        gate = jnp.dot(h, w_gu_ref[:, a:b], preferred_element_type=f32)
        up = jnp.dot(h, w_gu_ref[:, d_ff + a:d_ff + b], preferred_element_type=f32)
        act_ref[:, a:b] = (jax.nn.silu(gate) * up).astype(bf16)
    f = jnp.dot(act_ref[...], w_down_ref[...], preferred_element_type=f32)
    return x + _rmsnorm(f, g_post)


def _fill_ext_rows(v, ext_ref, keep_history, hist):
    n_lane_blocks, rows, _ = ext_ref.shape
    pad = _round_up(hist, SUBLANES)
    for j in range(n_lane_blocks):
        tail = ext_ref[j, rows - pad:, :]
        ext_ref[j, 0:pad, :] = jnp.where(keep_history, tail, 0.0)
        ext_ref[j, pad:, :] = v[:, _lanes(j)]


def _causal_dwconv_rows(ext_ref, w_ref, b_row, out_ref, hist):
    n_lane_blocks, rows, _ = ext_ref.shape
    pad = _round_up(hist, SUBLANES)
    base = pad - hist
    for r0 in range(0, rows - pad, CONV_ROWS):
        for j in range(n_lane_blocks):
            ls = _lanes(j)
            acc = ext_ref[j, base + r0:base + r0 + CONV_ROWS, :] * w_ref[0:1, ls]
            for k in range(1, hist + 1):
                acc = acc + ext_ref[j, base + r0 + k:base + r0 + k + CONV_ROWS, :] * w_ref[k:k + 1, ls]
            if b_row is not None:
                acc = acc + b_row[:, ls]
            out_ref[r0:r0 + CONV_ROWS, ls] = acc


def _fill_ext_slabs(v, ext_ref, state_ref, hist):
    n_lane_blocks, n_slabs, n_seq, _ = ext_ref.shape
    v3 = v.reshape(n_slabs - hist, n_seq, v.shape[-1])
    for j in range(n_lane_blocks):
        ext_ref[j, 0:hist] = state_ref[:, :, _lanes(j)]
        ext_ref[j, hist:] = v3[:, :, _lanes(j)]


def _causal_dwconv_slabs(ext_ref, w_ref, b_row, out_ref, hist):
    n_lane_blocks, n_slabs, n_seq, _ = ext_ref.shape
    new_slabs = n_slabs - hist
    for j in range(n_lane_blocks):
        ls = _lanes(j)
        for g0 in range(0, n_seq, SUBLANES):
            rows = slice(g0, g0 + SUBLANES)
            acc = ext_ref[j, 0:new_slabs, rows, :] * w_ref[0:1, ls]
            for k in range(1, hist + 1):
                acc = acc + ext_ref[j, k:k + new_slabs, rows, :] * w_ref[k:k + 1, ls]
            if b_row is not None:
                acc = acc + b_row[:, ls]
            for t in range(new_slabs):
                out_ref[t * n_seq + g0:t * n_seq + g0 + SUBLANES, ls] = acc[t]


def _layer_kernel(*refs, mixer, time_major, has_prev, steps_per_seq, n_tiles):
    it = iter(refs)
    x_ref = next(it)
    state_ref = next(it) if time_major else None
    if has_prev:
        next(it)
    vec = next(it)
    avec = next(it) if mixer == "a" else None
    w_in, w_conv, w_out, w_gu, w_down = [next(it) for _ in range(5)]
    y_ref, new_state_ref = next(it), next(it)
    ext_ref, c_ref, act_ref = next(it), next(it), next(it)

    step = pl.program_id(0)
    d = x_ref.shape[-1]
    hist = w_conv.shape[0] - 1
    n_lane_blocks = ext_ref.shape[0]

    def conv(v, bias):
        if time_major:
            _fill_ext_slabs(v, ext_ref, state_ref, hist)
            _causal_dwconv_slabs(ext_ref, w_conv, bias, c_ref, hist)
        else:
            _fill_ext_rows(v, ext_ref, step % steps_per_seq != 0, hist)
            _causal_dwconv_rows(ext_ref, w_conv, bias, c_ref, hist)
        return c_ref[...]

    @pl.when(step < n_tiles)
    def _():
        if not time_major:
            @pl.when(step == 0)
            def _():
                ext_ref[...] = jnp.zeros(ext_ref.shape, f32)

        x = x_ref[...].reshape(-1, d)
        h = _rmsnorm(x, vec[0:1, :]).astype(bf16)
        u = jnp.dot(h, w_in[...], preferred_element_type=f32)
        if mixer == "a":
            v = (u[:, :d] + avec[0:1, :]) * jax.nn.sigmoid(u[:, d:] + avec[1:2, :])
            c = _layernorm(conv(v, avec[2:3, :]), avec[3:4, :], avec[4:5, :])
            m = jnp.dot(jax.nn.silu(c).astype(bf16), w_out[...], preferred_element_type=f32)
            m = m + avec[5:6, :]
        else:
            c = conv(u[:, d:2 * d] * u[:, 2 * d:], None)
            m = jnp.dot((u[:, :d] * c).astype(bf16), w_out[...], preferred_element_type=f32)
        x = x + _rmsnorm(m, vec[1:2, :])
        y = _swiglu_residual(x, vec[2:3, :], w_gu, w_down, vec[3:4, :], act_ref)
        y_ref[...] = y.reshape(y_ref.shape)

        for j in range(n_lane_blocks):
            if time_major:
                n_slabs = ext_ref.shape[1]
                new_state_ref[:, :, _lanes(j)] = ext_ref[j, n_slabs - hist:]
            else:
                rows = ext_ref.shape[1]
                new_state_ref[:, _lanes(j)] = ext_ref[j, rows - hist:, :]

    @pl.when(step >= n_tiles)
    def _():
        new_state_ref[...] = jnp.zeros(new_state_ref.shape, f32)


def _layer_weights(arr, idx):
    nd = arr.ndim
    return pl.BlockSpec((None,) + arr.shape[1:], lambda n: (idx,) + (0,) * (nd - 1),
                        pipeline_mode=pl.Buffered(1))


def _run_layer(x, state_all, prev_state, vecs, avecs, w_in, w_conv, w_out, w_gu, w_down,
               *, mixer, layer, tile, name):
    d = x.shape[-1]
    j = layer // 2
    hist = w_conv.shape[1] - 1
    n_slabs = w_conv.shape[0]
    d_ff = w_down.shape[1]
    time_major = state_all is not None
    assert d % LANES == 0

    if time_major:
        seq_len, n_seqs, _ = x.shape
        assert n_seqs % tile == 0 and tile % SUBLANES == 0
        n_tiles = n_seqs // tile
        steps_per_seq = 1
        tile_rows = seq_len * tile
        last = n_tiles - 1
        x_spec = pl.BlockSpec((seq_len, tile, d), lambda n: (0, jnp.minimum(n, last), 0))
        state_block = (hist, tile, d)
        n_state_blocks = n_tiles
        state_index = lambda slab, blk: (slab, 0, blk, 0)
        block_of_tile = lambda t: t
        new_state_shape = (n_slabs, hist, n_seqs, d)
        ext_shape = (d // LANES, hist + seq_len, tile, LANES)
    else:
        n_batch, seq, _ = x.shape
        pad = _round_up(hist, SUBLANES)
        assert seq % tile == 0 and tile % CONV_ROWS == 0 and tile >= pad
        steps_per_seq = seq // tile
        n_tiles = n_batch * steps_per_seq
        tile_rows = tile
        last = n_tiles - 1
        x_spec = pl.BlockSpec(
            (None, tile, d),
            lambda n: (jnp.minimum(n, last) // steps_per_seq, jnp.minimum(n, last) % steps_per_seq, 0))
        state_block = (None, hist, d)
        n_state_blocks = n_batch
        state_index = lambda slab, blk: (slab, blk, 0, 0)
        block_of_tile = lambda t: t // steps_per_seq
        new_state_shape = (n_slabs, n_batch, hist, d)
        ext_shape = (d // LANES, pad + tile, LANES)

    in_specs, inputs = [x_spec], [x]
    if time_major:
        in_specs.append(pl.BlockSpec((None,) + state_block,
                                     lambda n: state_index(j, jnp.minimum(n, last))))
        inputs.append(state_all)

    aliases = {}
    if prev_state is None:
        assert j == 0
        n_fill = (n_slabs - 1) * n_state_blocks
    else:
        n_fill = 0
        aliases[len(inputs)] = 1
        in_specs.append(pl.BlockSpec(memory_space=pl.ANY))
        inputs.append(prev_state)

    def new_state_index(n):
        f = jnp.maximum(n - n_tiles, 0)
        slab = jnp.where(n < n_tiles, j, j + 1 + f // n_state_blocks)
        block = jnp.where(n < n_tiles, block_of_tile(jnp.minimum(n, last)), f % n_state_blocks)
        return state_index(slab, block)

    new_state_spec = pl.BlockSpec((None,) + state_block, new_state_index)

    in_specs.append(_layer_weights(vecs, layer))
    inputs.append(vecs)
    if mixer == "a":
        in_specs.append(_layer_weights(avecs, j))
        inputs.append(avecs)
    in_specs += [_layer_weights(w_in, j), _layer_weights(w_conv, j), _layer_weights(w_out, j),
                 _layer_weights(w_gu, layer), _layer_weights(w_down, layer)]
    inputs += [w_in, w_conv, w_out, w_gu, w_down]

    scratch = [
        pltpu.VMEM(ext_shape, f32),
        pltpu.VMEM((tile_rows, d), f32),
        pltpu.VMEM((tile_rows, d_ff), bf16),
    ]

    body = functools.partial(_layer_kernel, mixer=mixer, time_major=time_major,
                             has_prev=prev_state is not None,
                             steps_per_seq=steps_per_seq, n_tiles=n_tiles)
    return pl.pallas_call(
        body,
        grid=(n_tiles + n_fill,),
        in_specs=in_specs,
        out_specs=[x_spec, new_state_spec],
        out_shape=[jax.ShapeDtypeStruct(x.shape, x.dtype),
                   jax.ShapeDtypeStruct(new_state_shape, x.dtype)],
        input_output_aliases=aliases,
        scratch_shapes=scratch,
        compiler_params=pltpu.CompilerParams(
            dimension_semantics=("arbitrary",),
            vmem_limit_bytes=VMEM_LIMIT_BYTES),
        name=name,
    )(*inputs)


def kernel(x_prompt, x_sample, state_conv_a, state_conv_b, g_mix_pre, g_mix_post, g_ffn_pre, g_ffn_post, a_w_in, a_b_in, a_w_dw, a_b_dw, a_ln_g, a_ln_b, a_w_out, a_b_out, b_w_in, b_w_conv, b_w_out, w_gate_up, w_down):
    depth, d = g_mix_pre.shape
    n_a = a_w_in.shape[0]

    vecs = jnp.stack([g_mix_pre, g_mix_post, g_ffn_pre, g_ffn_post], axis=1)
    avecs = jnp.concatenate([a_b_in.reshape(n_a, 2, d), a_b_dw[:, None], a_ln_g[:, None],
                             a_ln_b[:, None], a_b_out[:, None]], axis=1)
    w_gu, w_dn = w_gate_up.astype(bf16), w_down.astype(bf16)
    mix = {"a": (a_w_in.astype(bf16), a_w_dw, a_w_out.astype(bf16)),
           "b": (b_w_in.astype(bf16), b_w_conv, b_w_out.astype(bf16))}
    states = {"a": jnp.swapaxes(state_conv_a, 1, 2), "b": jnp.swapaxes(state_conv_b, 1, 2)}

    xp = x_prompt
    xs = jnp.swapaxes(x_sample, 0, 1)
    new_p = {"a": None, "b": None}
    new_s = {"a": None, "b": None}
    for i in range(depth):
        m = "ab"[i % 2]
        w_in, w_conv, w_out = mix[m]
        common = dict(mixer=m, layer=i)
        xp, new_p[m] = _run_layer(xp, None, new_p[m], vecs, avecs, w_in, w_conv, w_out, w_gu, w_dn,
                                  tile=PROMPT_TILE, name=f"layer{i}_prompt", **common)
        xs, new_s[m] = _run_layer(xs, states[m], new_s[m], vecs, avecs, w_in, w_conv, w_out, w_gu, w_dn,
                                  tile=SAMPLE_SEQS[m], name=f"layer{i}_sample", **common)

    return (xp, jnp.swapaxes(xs, 0, 1), new_p["a"], new_p["b"],
            jnp.swapaxes(new_s["a"], 1, 2), jnp.swapaxes(new_s["b"], 1, 2))
```

```python
import functools

import jax
import jax.numpy as jnp
from jax import lax
from jax.experimental import pallas as pl
from jax.experimental.pallas import tpu as pltpu

RMS_EPS = 1e-6
LN_EPS = 1e-5

LANES = 128
SUBLANES = 8
MXU_COLS = 256
VMEM_LIMIT_BYTES = 56 * 1024 * 1024

PROMPT_TILE = 512
SAMPLE_SEQS = {"a": 32, "b": 64}
CONV_ROWS = 64
ROW_GROUPS = 2

f32 = jnp.float32
bf16 = jnp.bfloat16


def _round_up(n, m):
    return -(-n // m) * m


def _lanes(j):
    return slice(j * LANES, (j + 1) * LANES)


def _rmsnorm(x, g):
    ms = jnp.mean(x * x, axis=-1, keepdims=True)
    return x * lax.rsqrt(ms + RMS_EPS) * g


def _layernorm(x, g, b):
    mu = jnp.mean(x, axis=-1, keepdims=True)
    xc = x - mu
    var = jnp.mean(xc * xc, axis=-1, keepdims=True)
    return xc * lax.rsqrt(var + LN_EPS) * g + b


def _ffn_chunks(d_ff):
    step = 3 * MXU_COLS
    return [(a, min(a + step, d_ff)) for a in range(0, d_ff, step)]


def _swiglu_residual(xs, g_pre, w_gu_ref, w_down_ref, g_post, act_ref):
    d_ff = w_down_ref.shape[0]
    group = xs[0].shape[0]
    rows = [slice(i * group, (i + 1) * group) for i in range(len(xs))]
    hs = [_rmsnorm(x, g_pre).astype(bf16) for x in xs]
    for a, b in _ffn_chunks(d_ff):
        for h, r in zip(hs, rows):
            gate = jnp.dot(h, w_gu_ref[:, a:b], preferred_element_type=f32)
            up = jnp.dot(h, w_gu_ref[:, d_ff + a:d_ff + b], preferred_element_type=f32)
            act_ref[r, a:b] = (jax.nn.silu(gate) * up).astype(bf16)
    fs = [jnp.dot(act_ref[r, :], w_down_ref[...], preferred_element_type=f32) for r in rows]
    return [x + _rmsnorm(f, g_post) for x, f in zip(xs, fs)]


def _fill_ext_rows(v, ext_ref, keep_history, hist):
    n_lane_blocks, rows, _ = ext_ref.shape
    pad = _round_up(hist, SUBLANES)
    for j in range(n_lane_blocks):
        tail = ext_ref[j, rows - pad:, :]
        ext_ref[j, 0:pad, :] = jnp.where(keep_history, tail, 0.0)
        ext_ref[j, pad:, :] = v[:, _lanes(j)]


def _causal_dwconv_rows(ext_ref, w_ref, b_row, out_ref, hist):
    n_lane_blocks, rows, _ = ext_ref.shape
    pad = _round_up(hist, SUBLANES)
    base = pad - hist
    for r0 in range(0, rows - pad, CONV_ROWS):
        for j in range(n_lane_blocks):
            ls = _lanes(j)
            acc = ext_ref[j, base + r0:base + r0 + CONV_ROWS, :] * w_ref[0:1, ls]
            for k in range(1, hist + 1):
                acc = acc + ext_ref[j, base + r0 + k:base + r0 + k + CONV_ROWS, :] * w_ref[k:k + 1, ls]
            if b_row is not None:
                acc = acc + b_row[:, ls]
            out_ref[r0:r0 + CONV_ROWS, ls] = acc


def _fill_ext_slabs(v, ext_ref, state_ref, hist):
    n_lane_blocks, n_slabs, n_seq, _ = ext_ref.shape
    v3 = v.reshape(n_slabs - hist, n_seq, v.shape[-1])
    for j in range(n_lane_blocks):
        ext_ref[j, 0:hist] = state_ref[:, :, _lanes(j)]
        ext_ref[j, hist:] = v3[:, :, _lanes(j)]


def _causal_dwconv_slabs(ext_ref, w_ref, b_row, out_ref, hist):
    n_lane_blocks, n_slabs, n_seq, _ = ext_ref.shape
    new_slabs = n_slabs - hist
    for j in range(n_lane_blocks):
        ls = _lanes(j)
        for g0 in range(0, n_seq, SUBLANES):
            rows = slice(g0, g0 + SUBLANES)
            acc = ext_ref[j, 0:new_slabs, rows, :] * w_ref[0:1, ls]
            for k in range(1, hist + 1):
                acc = acc + ext_ref[j, k:k + new_slabs, rows, :] * w_ref[k:k + 1, ls]
            if b_row is not None:
                acc = acc + b_row[:, ls]
            for t in range(new_slabs):
                out_ref[t * n_seq + g0:t * n_seq + g0 + SUBLANES, ls] = acc[t]


def _layer_kernel(*refs, mixer, time_major, has_prev, steps_per_seq, n_tiles):
    it = iter(refs)
    x_ref = next(it)
    state_ref = next(it) if time_major else None
    if has_prev:
        next(it)
    vec = next(it)
    avec = next(it) if mixer == "a" else None
    w_in, w_conv, w_out, w_gu, w_down = [next(it) for _ in range(5)]
    y_ref, new_state_ref = next(it), next(it)
    ext_ref, c_ref, act_ref = next(it), next(it), next(it)

    step = pl.program_id(0)
    d = x_ref.shape[-1]
    hist = w_conv.shape[0] - 1
    n_lane_blocks = ext_ref.shape[0]

    def conv(v, bias):
        if time_major:
            _fill_ext_slabs(v, ext_ref, state_ref, hist)
            _causal_dwconv_slabs(ext_ref, w_conv, bias, c_ref, hist)
        else:
            _fill_ext_rows(v, ext_ref, step % steps_per_seq != 0, hist)
            _causal_dwconv_rows(ext_ref, w_conv, bias, c_ref, hist)
        return c_ref[...]

    @pl.when(step < n_tiles)
    def _():
        if not time_major:
            @pl.when(step == 0)
            def _():
                ext_ref[...] = jnp.zeros(ext_ref.shape, f32)

        x = x_ref[...].reshape(-1, d)
        n_rows = x.shape[0]
        n_groups = max(1, min(ROW_GROUPS, n_rows // MXU_COLS))
        group = n_rows // n_groups
        parts = lambda a: [a[i * group:(i + 1) * group] for i in range(n_groups)]
        join = lambda ps: jnp.concatenate(ps, axis=0)
        xs = parts(x)
        hs = [_rmsnorm(xi, vec[0:1, :]).astype(bf16) for xi in xs]
        us = [jnp.dot(hi, w_in[...], preferred_element_type=f32) for hi in hs]
        if mixer == "a":
            vs = [(u[:, :d] + avec[0:1, :]) * jax.nn.sigmoid(u[:, d:] + avec[1:2, :]) for u in us]
            cs = parts(conv(join(vs), avec[2:3, :]))
            cs = [_layernorm(c, avec[3:4, :], avec[4:5, :]) for c in cs]
            ms = [jnp.dot(jax.nn.silu(c).astype(bf16), w_out[...], preferred_element_type=f32)
                  + avec[5:6, :] for c in cs]
        else:
            cs = parts(conv(join([u[:, d:2 * d] * u[:, 2 * d:] for u in us]), None))
            ms = [jnp.dot((u[:, :d] * c).astype(bf16), w_out[...], preferred_element_type=f32)
                  for u, c in zip(us, cs)]
        xs = [xi + _rmsnorm(m, vec[1:2, :]) for xi, m in zip(xs, ms)]
        ys = _swiglu_residual(xs, vec[2:3, :], w_gu, w_down, vec[3:4, :], act_ref)
        y_ref[...] = join(ys).reshape(y_ref.shape)

        for j in range(n_lane_blocks):
            if time_major:
                n_slabs = ext_ref.shape[1]
                new_state_ref[:, :, _lanes(j)] = ext_ref[j, n_slabs - hist:]
            else:
                rows = ext_ref.shape[1]
                new_state_ref[:, _lanes(j)] = ext_ref[j, rows - hist:, :]

    @pl.when(step >= n_tiles)
    def _():
        new_state_ref[...] = jnp.zeros(new_state_ref.shape, f32)


def _layer_weights(arr, idx):
    nd = arr.ndim
    return pl.BlockSpec((None,) + arr.shape[1:], lambda n: (idx,) + (0,) * (nd - 1),
                        pipeline_mode=pl.Buffered(1))


def _run_layer(x, state_all, prev_state, vecs, avecs, w_in, w_conv, w_out, w_gu, w_down,
               *, mixer, layer, tile, name):
    d = x.shape[-1]
    j = layer // 2
    hist = w_conv.shape[1] - 1
    n_slabs = w_conv.shape[0]
    d_ff = w_down.shape[1]
    time_major = state_all is not None
    assert d % LANES == 0

    if time_major:
        seq_len, n_seqs, _ = x.shape
        assert n_seqs % tile == 0 and tile % SUBLANES == 0
        n_tiles = n_seqs // tile
        steps_per_seq = 1
        tile_rows = seq_len * tile
        last = n_tiles - 1
        x_spec = pl.BlockSpec((seq_len, tile, d), lambda n: (0, jnp.minimum(n, last), 0))
        state_block = (hist, tile, d)
        n_state_blocks = n_tiles
        state_index = lambda slab, blk: (slab, 0, blk, 0)
        block_of_tile = lambda t: t
        new_state_shape = (n_slabs, hist, n_seqs, d)
        ext_shape = (d // LANES, hist + seq_len, tile, LANES)
    else:
        n_batch, seq, _ = x.shape
        pad = _round_up(hist, SUBLANES)
        assert seq % tile == 0 and tile % CONV_ROWS == 0 and tile >= pad
        steps_per_seq = seq // tile
        n_tiles = n_batch * steps_per_seq
        tile_rows = tile
        last = n_tiles - 1
        x_spec = pl.BlockSpec(
            (None, tile, d),
            lambda n: (jnp.minimum(n, last) // steps_per_seq, jnp.minimum(n, last) % steps_per_seq, 0))
        state_block = (None, hist, d)
        n_state_blocks = n_batch
        state_index = lambda slab, blk: (slab, blk, 0, 0)
        block_of_tile = lambda t: t // steps_per_seq
        new_state_shape = (n_slabs, n_batch, hist, d)
        ext_shape = (d // LANES, pad + tile, LANES)

    in_specs, inputs = [x_spec], [x]
    if time_major:
        in_specs.append(pl.BlockSpec((None,) + state_block,
                                     lambda n: state_index(j, jnp.minimum(n, last))))
        inputs.append(state_all)

    aliases = {}
    if prev_state is None:
        assert j == 0
        n_fill = (n_slabs - 1) * n_state_blocks
    else:
        n_fill = 0
        aliases[len(inputs)] = 1
        in_specs.append(pl.BlockSpec(memory_space=pl.ANY))
        inputs.append(prev_state)

    def new_state_index(n):
        f = jnp.maximum(n - n_tiles, 0)
        slab = jnp.where(n < n_tiles, j, j + 1 + f // n_state_blocks)
        block = jnp.where(n < n_tiles, block_of_tile(jnp.minimum(n, last)), f % n_state_blocks)
        return state_index(slab, block)

    new_state_spec = pl.BlockSpec((None,) + state_block, new_state_index)

    in_specs.append(_layer_weights(vecs, layer))
    inputs.append(vecs)
    if mixer == "a":
        in_specs.append(_layer_weights(avecs, j))
        inputs.append(avecs)
    in_specs += [_layer_weights(w_in, j), _layer_weights(w_conv, j), _layer_weights(w_out, j),
                 _layer_weights(w_gu, layer), _layer_weights(w_down, layer)]
    inputs += [w_in, w_conv, w_out, w_gu, w_down]

    scratch = [
        pltpu.VMEM(ext_shape, f32),
        pltpu.VMEM((tile_rows, d), f32),
        pltpu.VMEM((tile_rows, d_ff), bf16),
    ]

    body = functools.partial(_layer_kernel, mixer=mixer, time_major=time_major,
                             has_prev=prev_state is not None,
                             steps_per_seq=steps_per_seq, n_tiles=n_tiles)
    return pl.pallas_call(
        body,
        grid=(n_tiles + n_fill,),
        in_specs=in_specs,
        out_specs=[x_spec, new_state_spec],
        out_shape=[jax.ShapeDtypeStruct(x.shape, x.dtype),
                   jax.ShapeDtypeStruct(new_state_shape, x.dtype)],
        input_output_aliases=aliases,
        scratch_shapes=scratch,
        compiler_params=pltpu.CompilerParams(
            dimension_semantics=("arbitrary",),
            vmem_limit_bytes=VMEM_LIMIT_BYTES),
        name=name,
    )(*inputs)


def kernel(x_prompt, x_sample, state_conv_a, state_conv_b, g_mix_pre, g_mix_post, g_ffn_pre, g_ffn_post, a_w_in, a_b_in, a_w_dw, a_b_dw, a_ln_g, a_ln_b, a_w_out, a_b_out, b_w_in, b_w_conv, b_w_out, w_gate_up, w_down):
    depth, d = g_mix_pre.shape
    n_a = a_w_in.shape[0]

    vecs = jnp.stack([g_mix_pre, g_mix_post, g_ffn_pre, g_ffn_post], axis=1)
    avecs = jnp.concatenate([a_b_in.reshape(n_a, 2, d), a_b_dw[:, None], a_ln_g[:, None],
                             a_ln_b[:, None], a_b_out[:, None]], axis=1)
    w_gu, w_dn = w_gate_up.astype(bf16), w_down.astype(bf16)
    mix = {"a": (a_w_in.astype(bf16), a_w_dw, a_w_out.astype(bf16)),
           "b": (b_w_in.astype(bf16), b_w_conv, b_w_out.astype(bf16))}
    states = {"a": jnp.swapaxes(state_conv_a, 1, 2), "b": jnp.swapaxes(state_conv_b, 1, 2)}

    xp = x_prompt
    xs = jnp.swapaxes(x_sample, 0, 1)
    new_p = {"a": None, "b": None}
    new_s = {"a": None, "b": None}
    for i in range(depth):
        m = "ab"[i % 2]
        w_in, w_conv, w_out = mix[m]
        common = dict(mixer=m, layer=i)
        xp, new_p[m] = _run_layer(xp, None, new_p[m], vecs, avecs, w_in, w_conv, w_out, w_gu, w_dn,
                                  tile=PROMPT_TILE, name=f"layer{i}_prompt", **common)
        xs, new_s[m] = _run_layer(xs, states[m], new_s[m], vecs, avecs, w_in, w_conv, w_out, w_gu, w_dn,
                                  tile=SAMPLE_SEQS[m], name=f"layer{i}_sample", **common)

    return (xp, jnp.swapaxes(xs, 0, 1), new_p["a"], new_p["b"],
            jnp.swapaxes(new_s["a"], 1, 2), jnp.swapaxes(new_s["b"], 1, 2))
```

```python
import functools

import jax
import jax.numpy as jnp
from jax import lax
from jax.experimental import pallas as pl
from jax.experimental.pallas import tpu as pltpu

RMS_EPS = 1e-6
LN_EPS = 1e-5

LANES = 128
SUBLANES = 8
MXU_COLS = 256
VMEM_LIMIT_BYTES = 56 * 1024 * 1024

PROMPT_TILE = 512
SAMPLE_SEQS = {"a": 32, "b": 64}
CONV_ROWS = 64
ROW_GROUPS = {"a": 4, "b": 2}
MIN_GROUP_ROWS = 128

f32 = jnp.float32
bf16 = jnp.bfloat16


def _round_up(n, m):
    return -(-n // m) * m


def _lanes(j):
    return slice(j * LANES, (j + 1) * LANES)


def _rmsnorm(x, g):
    ms = jnp.mean(x * x, axis=-1, keepdims=True)
    return x * lax.rsqrt(ms + RMS_EPS) * g


def _layernorm(x, g, b):
    mu = jnp.mean(x, axis=-1, keepdims=True)
    xc = x - mu
    var = jnp.mean(xc * xc, axis=-1, keepdims=True)
    return xc * lax.rsqrt(var + LN_EPS) * g + b


def _ffn_chunks(d_ff):
    step = 3 * MXU_COLS
    return [(a, min(a + step, d_ff)) for a in range(0, d_ff, step)]


def _swiglu_residual(xs, g_pre, w_gu_ref, w_down_ref, g_post, act_ref):
    d_ff = w_down_ref.shape[0]
    group = xs[0].shape[0]
    rows = [slice(i * group, (i + 1) * group) for i in range(len(xs))]
    hs = [_rmsnorm(x, g_pre).astype(bf16) for x in xs]
    for a, b in _ffn_chunks(d_ff):
        for h, r in zip(hs, rows):
            gate = jnp.dot(h, w_gu_ref[:, a:b], preferred_element_type=f32)
            up = jnp.dot(h, w_gu_ref[:, d_ff + a:d_ff + b], preferred_element_type=f32)
            act_ref[r, a:b] = (jax.nn.silu(gate) * up).astype(bf16)
    fs = [jnp.dot(act_ref[r, :], w_down_ref[...], preferred_element_type=f32) for r in rows]
    return [x + _rmsnorm(f, g_post) for x, f in zip(xs, fs)]


def _fill_ext_rows(v, ext_ref, keep_history, hist):
    n_lane_blocks, rows, _ = ext_ref.shape
    pad = _round_up(hist, SUBLANES)
    for j in range(n_lane_blocks):
        tail = ext_ref[j, rows - pad:, :]
        ext_ref[j, 0:pad, :] = jnp.where(keep_history, tail, 0.0)
        ext_ref[j, pad:, :] = v[:, _lanes(j)]


def _causal_dwconv_rows(ext_ref, w_ref, b_row, out_ref, hist):
    n_lane_blocks, rows, _ = ext_ref.shape
    pad = _round_up(hist, SUBLANES)
    base = pad - hist
    for r0 in range(0, rows - pad, CONV_ROWS):
        for j in range(n_lane_blocks):
            ls = _lanes(j)
            acc = ext_ref[j, base + r0:base + r0 + CONV_ROWS, :] * w_ref[0:1, ls]
            for k in range(1, hist + 1):
                acc = acc + ext_ref[j, base + r0 + k:base + r0 + k + CONV_ROWS, :] * w_ref[k:k + 1, ls]
            if b_row is not None:
                acc = acc + b_row[:, ls]
            out_ref[r0:r0 + CONV_ROWS, ls] = acc


def _fill_ext_slabs(v, ext_ref, state_ref, hist):
    n_lane_blocks, n_slabs, n_seq, _ = ext_ref.shape
    v3 = v.reshape(n_slabs - hist, n_seq, v.shape[-1])
    for j in range(n_lane_blocks):
        ext_ref[j, 0:hist] = state_ref[:, :, _lanes(j)]
        ext_ref[j, hist:] = v3[:, :, _lanes(j)]


def _causal_dwconv_slabs(ext_ref, w_ref, b_row, out_ref, hist):
    n_lane_blocks, n_slabs, n_seq, _ = ext_ref.shape
    new_slabs = n_slabs - hist
    for j in range(n_lane_blocks):
        ls = _lanes(j)
        for g0 in range(0, n_seq, SUBLANES):
            rows = slice(g0, g0 + SUBLANES)
            acc = ext_ref[j, 0:new_slabs, rows, :] * w_ref[0:1, ls]
            for k in range(1, hist + 1):
                acc = acc + ext_ref[j, k:k + new_slabs, rows, :] * w_ref[k:k + 1, ls]
            if b_row is not None:
                acc = acc + b_row[:, ls]
            for t in range(new_slabs):
                out_ref[t * n_seq + g0:t * n_seq + g0 + SUBLANES, ls] = acc[t]


def _layer_kernel(*refs, mixer, time_major, has_prev, steps_per_seq, n_tiles):
    it = iter(refs)
    x_ref = next(it)
    state_ref = next(it) if time_major else None
    if has_prev:
        next(it)
    vec = next(it)
    avec = next(it) if mixer == "a" else None
    w_in, w_conv, w_out, w_gu, w_down = [next(it) for _ in range(5)]
    y_ref, new_state_ref = next(it), next(it)
    ext_ref, c_ref, act_ref = next(it), next(it), next(it)

    step = pl.program_id(0)
    d = x_ref.shape[-1]
    hist = w_conv.shape[0] - 1
    n_lane_blocks = ext_ref.shape[0]

    def conv(v, bias):
        if time_major:
            _fill_ext_slabs(v, ext_ref, state_ref, hist)
            _causal_dwconv_slabs(ext_ref, w_conv, bias, c_ref, hist)
        else:
            _fill_ext_rows(v, ext_ref, step % steps_per_seq != 0, hist)
            _causal_dwconv_rows(ext_ref, w_conv, bias, c_ref, hist)
        return c_ref[...]

    @pl.when(step < n_tiles)
    def _():
        if not time_major:
            @pl.when(step == 0)
            def _():
                ext_ref[...] = jnp.zeros(ext_ref.shape, f32)

        x = x_ref[...].reshape(-1, d)
        n_rows = x.shape[0]
        n_groups = max(1, min(ROW_GROUPS[mixer], n_rows // MIN_GROUP_ROWS))
        group = n_rows // n_groups
        parts = lambda a: [a[i * group:(i + 1) * group] for i in range(n_groups)]
        join = lambda ps: jnp.concatenate(ps, axis=0)
        xs = parts(x)
        hs = [_rmsnorm(xi, vec[0:1, :]).astype(bf16) for xi in xs]
        us = [jnp.dot(hi, w_in[...], preferred_element_type=f32) for hi in hs]
        if mixer == "a":
            vs = [(u[:, :d] + avec[0:1, :]) * jax.nn.sigmoid(u[:, d:] + avec[1:2, :]) for u in us]
            cs = parts(conv(join(vs), avec[2:3, :]))
            cs = [_layernorm(c, avec[3:4, :], avec[4:5, :]) for c in cs]
            ms = [jnp.dot(jax.nn.silu(c).astype(bf16), w_out[...], preferred_element_type=f32)
                  + avec[5:6, :] for c in cs]
        else:
            cs = parts(conv(join([u[:, d:2 * d] * u[:, 2 * d:] for u in us]), None))
            ms = [jnp.dot((u[:, :d] * c).astype(bf16), w_out[...], preferred_element_type=f32)
                  for u, c in zip(us, cs)]
        xs = [xi + _rmsnorm(m, vec[1:2, :]) for xi, m in zip(xs, ms)]
        ys = _swiglu_residual(xs, vec[2:3, :], w_gu, w_down, vec[3:4, :], act_ref)
        y_ref[...] = join(ys).reshape(y_ref.shape)

        for j in range(n_lane_blocks):
            if time_major:
                n_slabs = ext_ref.shape[1]
                new_state_ref[:, :, _lanes(j)] = ext_ref[j, n_slabs - hist:]
            else:
                rows = ext_ref.shape[1]
                new_state_ref[:, _lanes(j)] = ext_ref[j, rows - hist:, :]

    @pl.when(step >= n_tiles)
    def _():
        new_state_ref[...] = jnp.zeros(new_state_ref.shape, f32)


def _layer_weights(arr, idx):
    nd = arr.ndim
    return pl.BlockSpec((None,) + arr.shape[1:], lambda n: (idx,) + (0,) * (nd - 1),
                        pipeline_mode=pl.Buffered(1))


def _run_layer(x, state_all, prev_state, vecs, avecs, w_in, w_conv, w_out, w_gu, w_down,
               *, mixer, layer, tile, name):
    d = x.shape[-1]
    j = layer // 2
    hist = w_conv.shape[1] - 1
    n_slabs = w_conv.shape[0]
    d_ff = w_down.shape[1]
    time_major = state_all is not None
    assert d % LANES == 0

    if time_major:
        seq_len, n_seqs, _ = x.shape
        assert n_seqs % tile == 0 and tile % SUBLANES == 0
        n_tiles = n_seqs // tile
        steps_per_seq = 1
        tile_rows = seq_len * tile
        last = n_tiles - 1
        x_spec = pl.BlockSpec((seq_len, tile, d), lambda n: (0, jnp.minimum(n, last), 0))
        state_block = (hist, tile, d)
        n_state_blocks = n_tiles
        state_index = lambda slab, blk: (slab, 0, blk, 0)
        block_of_tile = lambda t: t
        new_state_shape = (n_slabs, hist, n_seqs, d)
        ext_shape = (d // LANES, hist + seq_len, tile, LANES)
    else:
        n_batch, seq, _ = x.shape
        pad = _round_up(hist, SUBLANES)
        assert seq % tile == 0 and tile % CONV_ROWS == 0 and tile >= pad
        steps_per_seq = seq // tile
        n_tiles = n_batch * steps_per_seq
        tile_rows = tile
        last = n_tiles - 1
        x_spec = pl.BlockSpec(
            (None, tile, d),
            lambda n: (jnp.minimum(n, last) // steps_per_seq, jnp.minimum(n, last) % steps_per_seq, 0))
        state_block = (None, hist, d)
        n_state_blocks = n_batch
        state_index = lambda slab, blk: (slab, blk, 0, 0)
        block_of_tile = lambda t: t // steps_per_seq
        new_state_shape = (n_slabs, n_batch, hist, d)
        ext_shape = (d // LANES, pad + tile, LANES)

    in_specs, inputs = [x_spec], [x]
    if time_major:
        in_specs.append(pl.BlockSpec((None,) + state_block,
                                     lambda n: state_index(j, jnp.minimum(n, last))))
        inputs.append(state_all)

    aliases = {}
    if prev_state is None:
        assert j == 0
        n_fill = (n_slabs - 1) * n_state_blocks
    else:
        n_fill = 0
        aliases[len(inputs)] = 1
        in_specs.append(pl.BlockSpec(memory_space=pl.ANY))
        inputs.append(prev_state)

    def new_state_index(n):
        f = jnp.maximum(n - n_tiles, 0)
        slab = jnp.where(n < n_tiles, j, j + 1 + f // n_state_blocks)
        block = jnp.where(n < n_tiles, block_of_tile(jnp.minimum(n, last)), f % n_state_blocks)
        return state_index(slab, block)

    new_state_spec = pl.BlockSpec((None,) + state_block, new_state_index)

    in_specs.append(_layer_weights(vecs, layer))
    inputs.append(vecs)
    if mixer == "a":
        in_specs.append(_layer_weights(avecs, j))
        inputs.append(avecs)
    in_specs += [_layer_weights(w_in, j), _layer_weights(w_conv, j), _layer_weights(w_out, j),
                 _layer_weights(w_gu, layer), _layer_weights(w_down, layer)]
    inputs += [w_in, w_conv, w_out, w_gu, w_down]

    scratch = [
        pltpu.VMEM(ext_shape, f32),
        pltpu.VMEM((tile_rows, d), f32),
        pltpu.VMEM((tile_rows, d_ff), bf16),
    ]

    body = functools.partial(_layer_kernel, mixer=mixer, time_major=time_major,
                             has_prev=prev_state is not None,
                             steps_per_seq=steps_per_seq, n_tiles=n_tiles)
    return pl.pallas_call(
        body,
        grid=(n_tiles + n_fill,),
        in_specs=in_specs,
        out_specs=[x_spec, new_state_spec],
        out_shape=[jax.ShapeDtypeStruct(x.shape, x.dtype),
                   jax.ShapeDtypeStruct(new_state_shape, x.dtype)],
        input_output_aliases=aliases,
        scratch_shapes=scratch,
        compiler_params=pltpu.CompilerParams(
            dimension_semantics=("arbitrary",),
            vmem_limit_bytes=VMEM_LIMIT_BYTES),
        name=name,
    )(*inputs)


def kernel(x_prompt, x_sample, state_conv_a, state_conv_b, g_mix_pre, g_mix_post, g_ffn_pre, g_ffn_post, a_w_in, a_b_in, a_w_dw, a_b_dw, a_ln_g, a_ln_b, a_w_out, a_b_out, b_w_in, b_w_conv, b_w_out, w_gate_up, w_down):
    depth, d = g_mix_pre.shape
    n_a = a_w_in.shape[0]

    vecs = jnp.stack([g_mix_pre, g_mix_post, g_ffn_pre, g_ffn_post], axis=1)
    avecs = jnp.concatenate([a_b_in.reshape(n_a, 2, d), a_b_dw[:, None], a_ln_g[:, None],
                             a_ln_b[:, None], a_b_out[:, None]], axis=1)
    w_gu, w_dn = w_gate_up.astype(bf16), w_down.astype(bf16)
    mix = {"a": (a_w_in.astype(bf16), a_w_dw, a_w_out.astype(bf16)),
           "b": (b_w_in.astype(bf16), b_w_conv, b_w_out.astype(bf16))}
    states = {"a": jnp.swapaxes(state_conv_a, 1, 2), "b": jnp.swapaxes(state_conv_b, 1, 2)}

    xp = x_prompt
    xs = jnp.swapaxes(x_sample, 0, 1)
    new_p = {"a": None, "b": None}
    new_s = {"a": None, "b": None}
    for i in range(depth):
        m = "ab"[i % 2]
        w_in, w_conv, w_out = mix[m]
        common = dict(mixer=m, layer=i)
        xp, new_p[m] = _run_layer(xp, None, new_p[m], vecs, avecs, w_in, w_conv, w_out, w_gu, w_dn,
                                  tile=PROMPT_TILE, name=f"layer{i}_prompt", **common)
        xs, new_s[m] = _run_layer(xs, states[m], new_s[m], vecs, avecs, w_in, w_conv, w_out, w_gu, w_dn,
                                  tile=SAMPLE_SEQS[m], name=f"layer{i}_sample", **common)

    return (xp, jnp.swapaxes(xs, 0, 1), new_p["a"], new_p["b"],
            jnp.swapaxes(new_s["a"], 1, 2), jnp.swapaxes(new_s["b"], 1, 2))
```

```python
import functools

import jax
import jax.numpy as jnp
from jax import lax
from jax.experimental import pallas as pl
from jax.experimental.pallas import tpu as pltpu

RMS_EPS = 1e-6
LN_EPS = 1e-5

LANES = 128
SUBLANES = 8
MXU_COLS = 256
VMEM_LIMIT_BYTES = 56 * 1024 * 1024

PROMPT_TILE = 512
SAMPLE_SEQS = {"a": 32, "b": 64}
CONV_ROWS = 64
ROW_GROUPS = 2
MIN_GROUP_ROWS = 128

f32 = jnp.float32
bf16 = jnp.bfloat16


def _round_up(n, m):
    return -(-n // m) * m


def _lanes(j):
    return slice(j * LANES, (j + 1) * LANES)


def _rmsnorm(x, g):
    ms = jnp.mean(x * x, axis=-1, keepdims=True)
    return x * lax.rsqrt(ms + RMS_EPS) * g


def _layernorm(x, g, b):
    mu = jnp.mean(x, axis=-1, keepdims=True)
    xc = x - mu
    var = jnp.mean(xc * xc, axis=-1, keepdims=True)
    return xc * lax.rsqrt(var + LN_EPS) * g + b


def _ffn_chunks(d_ff):
    step = 3 * MXU_COLS
    return [(a, min(a + step, d_ff)) for a in range(0, d_ff, step)]


def _swiglu_residual(xs, g_pre, w_gu_ref, w_down_ref, g_post, act_ref):
    d_ff = w_down_ref.shape[0]
    group = xs[0].shape[0]
    rows = [slice(i * group, (i + 1) * group) for i in range(len(xs))]
    hs = [_rmsnorm(x, g_pre).astype(bf16) for x in xs]
    for a, b in _ffn_chunks(d_ff):
        for h, r in zip(hs, rows):
            gate = jnp.dot(h, w_gu_ref[:, a:b], preferred_element_type=f32)
            up = jnp.dot(h, w_gu_ref[:, d_ff + a:d_ff + b], preferred_element_type=f32)
            act_ref[r, a:b] = (jax.nn.silu(gate) * up).astype(bf16)
    fs = [jnp.dot(act_ref[r, :], w_down_ref[...], preferred_element_type=f32) for r in rows]
    return [x + _rmsnorm(f, g_post) for x, f in zip(xs, fs)]


def _fill_ext_rows(v, ext_ref, keep_history, hist):
    n_lane_blocks, rows, _ = ext_ref.shape
    pad = _round_up(hist, SUBLANES)
    for j in range(n_lane_blocks):
        tail = ext_ref[j, rows - pad:, :]
        ext_ref[j, 0:pad, :] = jnp.where(keep_history, tail, 0.0)
        ext_ref[j, pad:, :] = v[:, _lanes(j)]


def _causal_dwconv_rows(ext_ref, w_ref, b_row, out_ref, hist):
    n_lane_blocks, rows, _ = ext_ref.shape
    pad = _round_up(hist, SUBLANES)
    base = pad - hist
    for r0 in range(0, rows - pad, CONV_ROWS):
        for j in range(n_lane_blocks):
            ls = _lanes(j)
            acc = ext_ref[j, base + r0:base + r0 + CONV_ROWS, :] * w_ref[0:1, ls]
            for k in range(1, hist + 1):
                acc = acc + ext_ref[j, base + r0 + k:base + r0 + k + CONV_ROWS, :] * w_ref[k:k + 1, ls]
            if b_row is not None:
                acc = acc + b_row[:, ls]
            out_ref[r0:r0 + CONV_ROWS, ls] = acc


def _fill_ext_slabs(v, ext_ref, state_ref, hist):
    n_lane_blocks, n_slabs, n_seq, _ = ext_ref.shape
    v3 = v.reshape(n_slabs - hist, n_seq, v.shape[-1])
    for j in range(n_lane_blocks):
        ext_ref[j, 0:hist] = state_ref[:, :, _lanes(j)]
        ext_ref[j, hist:] = v3[:, :, _lanes(j)]


def _causal_dwconv_slabs(ext_ref, w_ref, b_row, out_ref, hist):
    n_lane_blocks, n_slabs, n_seq, _ = ext_ref.shape
    new_slabs = n_slabs - hist
    for j in range(n_lane_blocks):
        ls = _lanes(j)
        for g0 in range(0, n_seq, SUBLANES):
            rows = slice(g0, g0 + SUBLANES)
            acc = ext_ref[j, 0:new_slabs, rows, :] * w_ref[0:1, ls]
            for k in range(1, hist + 1):
                acc = acc + ext_ref[j, k:k + new_slabs, rows, :] * w_ref[k:k + 1, ls]
            if b_row is not None:
                acc = acc + b_row[:, ls]
            for t in range(new_slabs):
                out_ref[t * n_seq + g0:t * n_seq + g0 + SUBLANES, ls] = acc[t]


def _layer_kernel(*refs, mixer, time_major, has_prev, steps_per_seq, n_tiles):
    it = iter(refs)
    x_ref = next(it)
    state_ref = next(it) if time_major else None
    if has_prev:
        next(it)
    vec = next(it)
    avec = next(it) if mixer == "a" else None
    w_in, w_conv, w_out, w_gu, w_down = [next(it) for _ in range(5)]
    y_ref, new_state_ref = next(it), next(it)
    ext_ref, c_ref, act_ref = next(it), next(it), next(it)

    step = pl.program_id(0)
    d = x_ref.shape[-1]
    hist = w_conv.shape[0] - 1
    n_lane_blocks = ext_ref.shape[0]

    def conv(v, bias):
        if time_major:
            _fill_ext_slabs(v, ext_ref, state_ref, hist)
            _causal_dwconv_slabs(ext_ref, w_conv, bias, c_ref, hist)
        else:
            _fill_ext_rows(v, ext_ref, step % steps_per_seq != 0, hist)
            _causal_dwconv_rows(ext_ref, w_conv, bias, c_ref, hist)
        return c_ref[...]

    @pl.when(step < n_tiles)
    def _():
        if not time_major:
            @pl.when(step == 0)
            def _():
                ext_ref[...] = jnp.zeros(ext_ref.shape, f32)

        x = x_ref[...].reshape(-1, d)
        n_rows = x.shape[0]
        n_groups = max(1, min(ROW_GROUPS, n_rows // MIN_GROUP_ROWS))
        group = n_rows // n_groups
        parts = lambda a: [a[i * group:(i + 1) * group] for i in range(n_groups)]
        join = lambda ps: jnp.concatenate(ps, axis=0)
        xs = parts(x)
        hs = [_rmsnorm(xi, vec[0:1, :]).astype(bf16) for xi in xs]
        us = [jnp.dot(hi, w_in[...], preferred_element_type=f32) for hi in hs]
        if mixer == "a":
            vs = [(u[:, :d] + avec[0:1, :]) * jax.nn.sigmoid(u[:, d:] + avec[1:2, :]) for u in us]
            cs = parts(conv(join(vs), avec[2:3, :]))
            cs = [_layernorm(c, avec[3:4, :], avec[4:5, :]) for c in cs]
            ms = [jnp.dot(jax.nn.silu(c).astype(bf16), w_out[...], preferred_element_type=f32)
                  + avec[5:6, :] for c in cs]
        else:
            cs = parts(conv(join([u[:, d:2 * d] * u[:, 2 * d:] for u in us]), None))
            ms = [jnp.dot((u[:, :d] * c).astype(bf16), w_out[...], preferred_element_type=f32)
                  for u, c in zip(us, cs)]
        xs = [xi + _rmsnorm(m, vec[1:2, :]) for xi, m in zip(xs, ms)]
        ys = _swiglu_residual(xs, vec[2:3, :], w_gu, w_down, vec[3:4, :], act_ref)
        y_ref[...] = join(ys).reshape(y_ref.shape)

        for j in range(n_lane_blocks):
            if time_major:
                n_slabs = ext_ref.shape[1]
                new_state_ref[:, :, _lanes(j)] = ext_ref[j, n_slabs - hist:]
            else:
                rows = ext_ref.shape[1]
                new_state_ref[:, _lanes(j)] = ext_ref[j, rows - hist:, :]

    @pl.when(step >= n_tiles)
    def _():
        new_state_ref[...] = jnp.zeros(new_state_ref.shape, f32)


def _layer_weights(arr, idx):
    nd = arr.ndim
    return pl.BlockSpec((None,) + arr.shape[1:], lambda n: (idx,) + (0,) * (nd - 1),
                        pipeline_mode=pl.Buffered(1))


def _run_layer(x, state_all, prev_state, vecs, avecs, w_in, w_conv, w_out, w_gu, w_down,
               *, mixer, layer, tile, name):
    d = x.shape[-1]
    j = layer // 2
    hist = w_conv.shape[1] - 1
    n_slabs = w_conv.shape[0]
    d_ff = w_down.shape[1]
    time_major = state_all is not None
    assert d % LANES == 0

    if time_major:
        seq_len, n_seqs, _ = x.shape
        assert n_seqs % tile == 0 and tile % SUBLANES == 0
        n_tiles = n_seqs // tile
        steps_per_seq = 1
        tile_rows = seq_len * tile
        last = n_tiles - 1
        x_spec = pl.BlockSpec((seq_len, tile, d), lambda n: (0, jnp.minimum(n, last), 0))
        state_block = (hist, tile, d)
        n_state_blocks = n_tiles
        state_index = lambda slab, blk: (slab, 0, blk, 0)
        block_of_tile = lambda t: t
        new_state_shape = (n_slabs, hist, n_seqs, d)
        ext_shape = (d // LANES, hist + seq_len, tile, LANES)
    else:
        n_batch, seq, _ = x.shape
        pad = _round_up(hist, SUBLANES)
        assert seq % tile == 0 and tile % CONV_ROWS == 0 and tile >= pad
        steps_per_seq = seq // tile
        n_tiles = n_batch * steps_per_seq
        tile_rows = tile
        last = n_tiles - 1
        x_spec = pl.BlockSpec(
            (None, tile, d),
            lambda n: (jnp.minimum(n, last) // steps_per_seq, jnp.minimum(n, last) % steps_per_seq, 0))
        state_block = (None, hist, d)
        n_state_blocks = n_batch
        state_index = lambda slab, blk: (slab, blk, 0, 0)
        block_of_tile = lambda t: t // steps_per_seq
        new_state_shape = (n_slabs, n_batch, hist, d)
        ext_shape = (d // LANES, pad + tile, LANES)

    in_specs, inputs = [x_spec], [x]
    if time_major:
        in_specs.append(pl.BlockSpec((None,) + state_block,
                                     lambda n: state_index(j, jnp.minimum(n, last))))
        inputs.append(state_all)

    aliases = {}
    if prev_state is None:
        assert j == 0
        n_fill = (n_slabs - 1) * n_state_blocks
    else:
        n_fill = 0
        aliases[len(inputs)] = 1
        in_specs.append(pl.BlockSpec(memory_space=pl.ANY))
        inputs.append(prev_state)

    def new_state_index(n):
        f = jnp.maximum(n - n_tiles, 0)
        slab = jnp.where(n < n_tiles, j, j + 1 + f // n_state_blocks)
        block = jnp.where(n < n_tiles, block_of_tile(jnp.minimum(n, last)), f % n_state_blocks)
        return state_index(slab, block)

    new_state_spec = pl.BlockSpec((None,) + state_block, new_state_index)

    in_specs.append(_layer_weights(vecs, layer))
    inputs.append(vecs)
    if mixer == "a":
        in_specs.append(_layer_weights(avecs, j))
        inputs.append(avecs)
    in_specs += [_layer_weights(w_in, j), _layer_weights(w_conv, j), _layer_weights(w_out, j),
                 _layer_weights(w_gu, layer), _layer_weights(w_down, layer)]
    inputs += [w_in, w_conv, w_out, w_gu, w_down]

    scratch = [
        pltpu.VMEM(ext_shape, f32),
        pltpu.VMEM((tile_rows, d), f32),
        pltpu.VMEM((tile_rows, d_ff), bf16),
    ]

    body = functools.partial(_layer_kernel, mixer=mixer, time_major=time_major,
                             has_prev=prev_state is not None,
                             steps_per_seq=steps_per_seq, n_tiles=n_tiles)
    return pl.pallas_call(
        body,
        grid=(n_tiles + n_fill,),
        in_specs=in_specs,
        out_specs=[x_spec, new_state_spec],
        out_shape=[jax.ShapeDtypeStruct(x.shape, x.dtype),
                   jax.ShapeDtypeStruct(new_state_shape, x.dtype)],
        input_output_aliases=aliases,
        scratch_shapes=scratch,
        compiler_params=pltpu.CompilerParams(
            dimension_semantics=("arbitrary",),
            vmem_limit_bytes=VMEM_LIMIT_BYTES),
        name=name,
    )(*inputs)


def kernel(x_prompt, x_sample, state_conv_a, state_conv_b, g_mix_pre, g_mix_post, g_ffn_pre, g_ffn_post, a_w_in, a_b_in, a_w_dw, a_b_dw, a_ln_g, a_ln_b, a_w_out, a_b_out, b_w_in, b_w_conv, b_w_out, w_gate_up, w_down):
    depth, d = g_mix_pre.shape
    n_a = a_w_in.shape[0]

    vecs = jnp.stack([g_mix_pre, g_mix_post, g_ffn_pre, g_ffn_post], axis=1)
    avecs = jnp.concatenate([a_b_in.reshape(n_a, 2, d), a_b_dw[:, None], a_ln_g[:, None],
                             a_ln_b[:, None], a_b_out[:, None]], axis=1)
    w_gu, w_dn = w_gate_up.astype(bf16), w_down.astype(bf16)
    mix = {"a": (a_w_in.astype(bf16), a_w_dw, a_w_out.astype(bf16)),
           "b": (b_w_in.astype(bf16), b_w_conv, b_w_out.astype(bf16))}
    states = {"a": jnp.swapaxes(state_conv_a, 1, 2), "b": jnp.swapaxes(state_conv_b, 1, 2)}

    xp = x_prompt
    xs = jnp.swapaxes(x_sample, 0, 1)
    new_p = {"a": None, "b": None}
    new_s = {"a": None, "b": None}
    for i in range(depth):
        m = "ab"[i % 2]
        w_in, w_conv, w_out = mix[m]
        common = dict(mixer=m, layer=i)
        xp, new_p[m] = _run_layer(xp, None, new_p[m], vecs, avecs, w_in, w_conv, w_out, w_gu, w_dn,
                                  tile=PROMPT_TILE, name=f"layer{i}_prompt", **common)
        xs, new_s[m] = _run_layer(xs, states[m], new_s[m], vecs, avecs, w_in, w_conv, w_out, w_gu, w_dn,
                                  tile=SAMPLE_SEQS[m], name=f"layer{i}_sample", **common)

    return (xp, jnp.swapaxes(xs, 0, 1), new_p["a"], new_p["b"],
            jnp.swapaxes(new_s["a"], 1, 2), jnp.swapaxes(new_s["b"], 1, 2))
```

```python
import functools

import jax
import jax.numpy as jnp
from jax import lax
from jax.experimental import pallas as pl
from jax.experimental.pallas import tpu as pltpu

RMS_EPS = 1e-6
LN_EPS = 1e-5

LANES = 128
SUBLANES = 8
MXU_COLS = 256
VMEM_LIMIT_BYTES = 62 * 1024 * 1024

PROMPT_TILE = {"a": 512, "b": 1024}
SAMPLE_SEQS = {"a": 32, "b": 64}
CONV_ROWS = 64
ROW_GROUPS = 2
MIN_GROUP_ROWS = 128

f32 = jnp.float32
bf16 = jnp.bfloat16


def _round_up(n, m):
    return -(-n // m) * m


def _lanes(j):
    return slice(j * LANES, (j + 1) * LANES)


def _rmsnorm(x, g):
    ms = jnp.mean(x * x, axis=-1, keepdims=True)
    return x * lax.rsqrt(ms + RMS_EPS) * g


def _layernorm(x, g, b):
    mu = jnp.mean(x, axis=-1, keepdims=True)
    xc = x - mu
    var = jnp.mean(xc * xc, axis=-1, keepdims=True)
    return xc * lax.rsqrt(var + LN_EPS) * g + b


def _ffn_chunks(d_ff):
    step = 3 * MXU_COLS
    return [(a, min(a + step, d_ff)) for a in range(0, d_ff, step)]


def _swiglu_residual(xs, g_pre, w_gu_ref, w_down_ref, g_post, act_ref):
    d_ff = w_down_ref.shape[0]
    group = xs[0].shape[0]
    rows = [slice(i * group, (i + 1) * group) for i in range(len(xs))]
    hs = [_rmsnorm(x, g_pre).astype(bf16) for x in xs]
    for a, b in _ffn_chunks(d_ff):
        for h, r in zip(hs, rows):
            gate = jnp.dot(h, w_gu_ref[:, a:b], preferred_element_type=f32)
            up = jnp.dot(h, w_gu_ref[:, d_ff + a:d_ff + b], preferred_element_type=f32)
            act_ref[r, a:b] = (jax.nn.silu(gate) * up).astype(bf16)
    fs = [jnp.dot(act_ref[r, :], w_down_ref[...], preferred_element_type=f32) for r in rows]
    return [x + _rmsnorm(f, g_post) for x, f in zip(xs, fs)]


def _fill_ext_rows(v, ext_ref, keep_history, hist):
    n_lane_blocks, rows, _ = ext_ref.shape
    pad = _round_up(hist, SUBLANES)
    for j in range(n_lane_blocks):
        tail = ext_ref[j, rows - pad:, :]
        ext_ref[j, 0:pad, :] = jnp.where(keep_history, tail, 0.0)
        ext_ref[j, pad:, :] = v[:, _lanes(j)]


def _causal_dwconv_rows(ext_ref, w_ref, b_row, out_ref, hist):
    n_lane_blocks, rows, _ = ext_ref.shape
    pad = _round_up(hist, SUBLANES)
    base = pad - hist
    for r0 in range(0, rows - pad, CONV_ROWS):
        for j in range(n_lane_blocks):
            ls = _lanes(j)
            acc = ext_ref[j, base + r0:base + r0 + CONV_ROWS, :] * w_ref[0:1, ls]
            for k in range(1, hist + 1):
                acc = acc + ext_ref[j, base + r0 + k:base + r0 + k + CONV_ROWS, :] * w_ref[k:k + 1, ls]
            if b_row is not None:
                acc = acc + b_row[:, ls]
            out_ref[r0:r0 + CONV_ROWS, ls] = acc


def _fill_ext_slabs(v, ext_ref, state_ref, hist):
    n_lane_blocks, n_slabs, n_seq, _ = ext_ref.shape
    v3 = v.reshape(n_slabs - hist, n_seq, v.shape[-1])
    for j in range(n_lane_blocks):
        ext_ref[j, 0:hist] = state_ref[:, :, _lanes(j)]
        ext_ref[j, hist:] = v3[:, :, _lanes(j)]


def _causal_dwconv_slabs(ext_ref, w_ref, b_row, out_ref, hist):
    n_lane_blocks, n_slabs, n_seq, _ = ext_ref.shape
    new_slabs = n_slabs - hist
    for j in range(n_lane_blocks):
        ls = _lanes(j)
        for g0 in range(0, n_seq, SUBLANES):
            rows = slice(g0, g0 + SUBLANES)
            acc = ext_ref[j, 0:new_slabs, rows, :] * w_ref[0:1, ls]
            for k in range(1, hist + 1):
                acc = acc + ext_ref[j, k:k + new_slabs, rows, :] * w_ref[k:k + 1, ls]
            if b_row is not None:
                acc = acc + b_row[:, ls]
            for t in range(new_slabs):
                out_ref[t * n_seq + g0:t * n_seq + g0 + SUBLANES, ls] = acc[t]


def _layer_kernel(*refs, mixer, time_major, has_prev, steps_per_seq, n_tiles):
    it = iter(refs)
    x_ref = next(it)
    state_ref = next(it) if time_major else None
    if has_prev:
        next(it)
    vec = next(it)
    avec = next(it) if mixer == "a" else None
    w_in, w_conv, w_out, w_gu, w_down = [next(it) for _ in range(5)]
    y_ref, new_state_ref = next(it), next(it)
    ext_ref, c_ref, act_ref = next(it), next(it), next(it)

    step = pl.program_id(0)
    d = x_ref.shape[-1]
    hist = w_conv.shape[0] - 1
    n_lane_blocks = ext_ref.shape[0]

    def conv(v, bias):
        if time_major:
            _fill_ext_slabs(v, ext_ref, state_ref, hist)
            _causal_dwconv_slabs(ext_ref, w_conv, bias, c_ref, hist)
        else:
            _fill_ext_rows(v, ext_ref, step % steps_per_seq != 0, hist)
            _causal_dwconv_rows(ext_ref, w_conv, bias, c_ref, hist)
        return c_ref[...]

    @pl.when(step < n_tiles)
    def _():
        if not time_major:
            @pl.when(step == 0)
            def _():
                ext_ref[...] = jnp.zeros(ext_ref.shape, f32)

        x = x_ref[...].reshape(-1, d)
        n_rows = x.shape[0]
        n_groups = max(1, min(ROW_GROUPS, n_rows // MIN_GROUP_ROWS))
        group = n_rows // n_groups
        parts = lambda a: [a[i * group:(i + 1) * group] for i in range(n_groups)]
        join = lambda ps: jnp.concatenate(ps, axis=0)
        xs = parts(x)
        hs = [_rmsnorm(xi, vec[0:1, :]).astype(bf16) for xi in xs]
        us = [jnp.dot(hi, w_in[...], preferred_element_type=f32) for hi in hs]
        if mixer == "a":
            vs = [(u[:, :d] + avec[0:1, :]) * jax.nn.sigmoid(u[:, d:] + avec[1:2, :]) for u in us]
            cs = parts(conv(join(vs), avec[2:3, :]))
            cs = [_layernorm(c, avec[3:4, :], avec[4:5, :]) for c in cs]
            ms = [jnp.dot(jax.nn.silu(c).astype(bf16), w_out[...], preferred_element_type=f32)
                  + avec[5:6, :] for c in cs]
        else:
            cs = parts(conv(join([u[:, d:2 * d] * u[:, 2 * d:] for u in us]), None))
            ms = [jnp.dot((u[:, :d] * c).astype(bf16), w_out[...], preferred_element_type=f32)
                  for u, c in zip(us, cs)]
        xs = [xi + _rmsnorm(m, vec[1:2, :]) for xi, m in zip(xs, ms)]
        ys = _swiglu_residual(xs, vec[2:3, :], w_gu, w_down, vec[3:4, :], act_ref)
        y_ref[...] = join(ys).reshape(y_ref.shape)

        for j in range(n_lane_blocks):
            if time_major:
                n_slabs = ext_ref.shape[1]
                new_state_ref[:, :, _lanes(j)] = ext_ref[j, n_slabs - hist:]
            else:
                rows = ext_ref.shape[1]
                new_state_ref[:, _lanes(j)] = ext_ref[j, rows - hist:, :]

    @pl.when(step >= n_tiles)
    def _():
        new_state_ref[...] = jnp.zeros(new_state_ref.shape, f32)


def _layer_weights(arr, idx):
    nd = arr.ndim
    return pl.BlockSpec((None,) + arr.shape[1:], lambda n: (idx,) + (0,) * (nd - 1),
                        pipeline_mode=pl.Buffered(1))


def _run_layer(x, state_all, prev_state, vecs, avecs, w_in, w_conv, w_out, w_gu, w_down,
               *, mixer, layer, tile, name):
    d = x.shape[-1]
    j = layer // 2
    hist = w_conv.shape[1] - 1
    n_slabs = w_conv.shape[0]
    d_ff = w_down.shape[1]
    time_major = state_all is not None
    assert d % LANES == 0

    if time_major:
        seq_len, n_seqs, _ = x.shape
        assert n_seqs % tile == 0 and tile % SUBLANES == 0
        n_tiles = n_seqs // tile
        steps_per_seq = 1
        tile_rows = seq_len * tile
        last = n_tiles - 1
        x_spec = pl.BlockSpec((seq_len, tile, d), lambda n: (0, jnp.minimum(n, last), 0))
        state_block = (hist, tile, d)
        n_state_blocks = n_tiles
        state_index = lambda slab, blk: (slab, 0, blk, 0)
        block_of_tile = lambda t: t
        new_state_shape = (n_slabs, hist, n_seqs, d)
        ext_shape = (d // LANES, hist + seq_len, tile, LANES)
    else:
        n_batch, seq, _ = x.shape
        pad = _round_up(hist, SUBLANES)
        assert seq % tile == 0 and tile % CONV_ROWS == 0 and tile >= pad
        steps_per_seq = seq // tile
        n_tiles = n_batch * steps_per_seq
        tile_rows = tile
        last = n_tiles - 1
        x_spec = pl.BlockSpec(
            (None, tile, d),
            lambda n: (jnp.minimum(n, last) // steps_per_seq, jnp.minimum(n, last) % steps_per_seq, 0))
        state_block = (None, hist, d)
        n_state_blocks = n_batch
        state_index = lambda slab, blk: (slab, blk, 0, 0)
        block_of_tile = lambda t: t // steps_per_seq
        new_state_shape = (n_slabs, n_batch, hist, d)
        ext_shape = (d // LANES, pad + tile, LANES)

    in_specs, inputs = [x_spec], [x]
    if time_major:
        in_specs.append(pl.BlockSpec((None,) + state_block,
                                     lambda n: state_index(j, jnp.minimum(n, last))))
        inputs.append(state_all)

    aliases = {}
    if prev_state is None:
        assert j == 0
        n_fill = (n_slabs - 1) * n_state_blocks
    else:
        n_fill = 0
        aliases[len(inputs)] = 1
        in_specs.append(pl.BlockSpec(memory_space=pl.ANY))
        inputs.append(prev_state)

    def new_state_index(n):
        f = jnp.maximum(n - n_tiles, 0)
        slab = jnp.where(n < n_tiles, j, j + 1 + f // n_state_blocks)
        block = jnp.where(n < n_tiles, block_of_tile(jnp.minimum(n, last)), f % n_state_blocks)
        return state_index(slab, block)

    new_state_spec = pl.BlockSpec((None,) + state_block, new_state_index)

    in_specs.append(_layer_weights(vecs, layer))
    inputs.append(vecs)
    if mixer == "a":
        in_specs.append(_layer_weights(avecs, j))
        inputs.append(avecs)
    in_specs += [_layer_weights(w_in, j), _layer_weights(w_conv, j), _layer_weights(w_out, j),
                 _layer_weights(w_gu, layer), _layer_weights(w_down, layer)]
    inputs += [w_in, w_conv, w_out, w_gu, w_down]

    scratch = [
        pltpu.VMEM(ext_shape, f32),
        pltpu.VMEM((tile_rows, d), f32),
        pltpu.VMEM((tile_rows, d_ff), bf16),
    ]

    body = functools.partial(_layer_kernel, mixer=mixer, time_major=time_major,
                             has_prev=prev_state is not None,
                             steps_per_seq=steps_per_seq, n_tiles=n_tiles)
    return pl.pallas_call(
        body,
        grid=(n_tiles + n_fill,),
        in_specs=in_specs,
        out_specs=[x_spec, new_state_spec],
        out_shape=[jax.ShapeDtypeStruct(x.shape, x.dtype),
                   jax.ShapeDtypeStruct(new_state_shape, x.dtype)],
        input_output_aliases=aliases,
        scratch_shapes=scratch,
        compiler_params=pltpu.CompilerParams(
            dimension_semantics=("arbitrary",),
            vmem_limit_bytes=VMEM_LIMIT_BYTES),
        name=name,
    )(*inputs)


def kernel(x_prompt, x_sample, state_conv_a, state_conv_b, g_mix_pre, g_mix_post, g_ffn_pre, g_ffn_post, a_w_in, a_b_in, a_w_dw, a_b_dw, a_ln_g, a_ln_b, a_w_out, a_b_out, b_w_in, b_w_conv, b_w_out, w_gate_up, w_down):
    depth, d = g_mix_pre.shape
    n_a = a_w_in.shape[0]

    vecs = jnp.stack([g_mix_pre, g_mix_post, g_ffn_pre, g_ffn_post], axis=1)
    avecs = jnp.concatenate([a_b_in.reshape(n_a, 2, d), a_b_dw[:, None], a_ln_g[:, None],
                             a_ln_b[:, None], a_b_out[:, None]], axis=1)
    w_gu, w_dn = w_gate_up.astype(bf16), w_down.astype(bf16)
    mix = {"a": (a_w_in.astype(bf16), a_w_dw, a_w_out.astype(bf16)),
           "b": (b_w_in.astype(bf16), b_w_conv, b_w_out.astype(bf16))}
    states = {"a": jnp.swapaxes(state_conv_a, 1, 2), "b": jnp.swapaxes(state_conv_b, 1, 2)}

    xp = x_prompt
    xs = jnp.swapaxes(x_sample, 0, 1)
    new_p = {"a": None, "b": None}
    new_s = {"a": None, "b": None}
    for i in range(depth):
        m = "ab"[i % 2]
        w_in, w_conv, w_out = mix[m]
        common = dict(mixer=m, layer=i)
        xp, new_p[m] = _run_layer(xp, None, new_p[m], vecs, avecs, w_in, w_conv, w_out, w_gu, w_dn,
                                  tile=PROMPT_TILE[m], name=f"layer{i}_prompt", **common)
        xs, new_s[m] = _run_layer(xs, states[m], new_s[m], vecs, avecs, w_in, w_conv, w_out, w_gu, w_dn,
                                  tile=SAMPLE_SEQS[m], name=f"layer{i}_sample", **common)

    return (xp, jnp.swapaxes(xs, 0, 1), new_p["a"], new_p["b"],
            jnp.swapaxes(new_s["a"], 1, 2), jnp.swapaxes(new_s["b"], 1, 2))
```
